```python
import jax, jax.numpy as jnp
from jax import lax
import numpy as np

D_MODEL = 1024
BATCH = 4
SEQ = 8192
DEPTH = 1

HEAD_DIM = 64
N_Q_HEADS = D_MODEL // HEAD_DIM
N_KV_HEADS = N_Q_HEADS // 4
GQA_GROUP = N_Q_HEADS // N_KV_HEADS
ATTN_WIDTH = N_Q_HEADS * HEAD_DIM
KV_WIDTH = N_KV_HEADS * HEAD_DIM
WINDOW = 128
BLOCK = 128
CONV_WIDTH = D_MODEL
CONV_K = 3
D_FF = -(-8 * D_MODEL // (3 * 256)) * 256
SPLIT_SIZES = (ATTN_WIDTH, KV_WIDTH, KV_WIDTH, CONV_WIDTH, CONV_WIDTH, CONV_WIDTH, D_MODEL, D_MODEL)
IN_WIDTH = sum(SPLIT_SIZES)
SPLIT_POINTS = np.cumsum(SPLIT_SIZES)[:-1].tolist()
ALPHA = (2.0 * DEPTH) ** 0.25
BETA = (8.0 * DEPTH) ** -0.25
LN_EPS = 1e-5
NEG_INF = -1e30

kernel_name = "hybrid_shortconv_swa_sink_alibi_deepnorm"


def layer_norm(x, g, b):
    xf = x.astype(jnp.float32)
    mu = jnp.mean(xf, axis=-1, keepdims=True)
    var = jnp.mean(jnp.square(xf - mu), axis=-1, keepdims=True)
    y = (xf - mu) * lax.rsqrt(var + LN_EPS) * g.astype(jnp.float32) + b.astype(jnp.float32)
    return y.astype(x.dtype)


def alibi_slopes(n_heads):
    return np.array([2.0 ** (-8.0 * (i + 1) / n_heads) for i in range(n_heads)], dtype=np.float32)


def short_conv_mixer(b_gate, c_gate, h, conv_w):
    u = c_gate * h
    rhs = conv_w[:, None, :].astype(u.dtype)
    y = lax.conv_general_dilated(u, rhs, window_strides=(1,), padding=[(CONV_K - 1, 0)],
                                 dimension_numbers=('NWC', 'WIO', 'NWC'),
                                 feature_group_count=CONV_WIDTH)
    return b_gate * y


def sliding_window_attention(q, k, v, sinks):
    bsz, seq = q.shape[0], q.shape[1]
    nblk = seq // BLOCK
    qb = q.reshape(bsz, nblk, BLOCK, N_KV_HEADS, GQA_GROUP, HEAD_DIM)

    def band(t):
        tb = t.reshape(bsz, nblk, BLOCK, N_KV_HEADS, HEAD_DIM)
        prev = jnp.pad(tb[:, :-1], ((0, 0), (1, 0), (0, 0), (0, 0), (0, 0)))
        return jnp.concatenate([prev, tb], axis=2)

    kb, vb = band(k), band(v)
    scores = jnp.einsum('bnqkgd,bnskd->bnkgqs', qb, kb).astype(jnp.float32) * (HEAD_DIM ** -0.5)

    qi = np.arange(BLOCK)[:, None]
    kj = np.arange(2 * BLOCK)[None, :]
    dist = BLOCK + qi - kj
    in_window = (dist >= 0) & (dist < WINDOW)
    key_pos = (np.arange(nblk)[:, None] - 1) * BLOCK + np.arange(2 * BLOCK)[None, :]
    valid = in_window[None] & (key_pos >= 0)[:, None, :]
    slopes = alibi_slopes(N_Q_HEADS).reshape(N_KV_HEADS, GQA_GROUP)
    bias = -slopes[:, :, None, None] * np.abs(dist).astype(np.float32)[None, None]

    scores = jnp.where(valid[None, :, None, None], scores + bias, NEG_INF)
    sink = sinks.astype(jnp.float32).reshape(N_KV_HEADS, GQA_GROUP)[:, :, None, None]
    m = jnp.maximum(jnp.max(scores, axis=-1, keepdims=True), sink)
    p = jnp.exp(scores - m)
    probs = p / (jnp.sum(p, axis=-1, keepdims=True) + jnp.exp(sink - m))
    out = jnp.einsum('bnkgqs,bnskd->bnqkgd', probs.astype(v.dtype), vb)
    return out.reshape(bsz, seq, ATTN_WIDTH)


def setup_inputs(seed: int = 0) -> dict:
    key = jax.random.key(seed)
    ks = jax.random.split(key, 13)
    f32 = jnp.float32
    x = jax.random.normal(ks[0], (BATCH, SEQ, D_MODEL), f32)
    w_in = jax.random.normal(ks[1], (DEPTH, D_MODEL, IN_WIDTH), f32) * D_MODEL ** -0.5
    conv_w = jax.random.normal(ks[2], (DEPTH, CONV_K, CONV_WIDTH), f32) * CONV_K ** -0.5
    attn_sinks = jax.random.normal(ks[3], (DEPTH, N_Q_HEADS), f32) * 0.5
    w_out = jax.random.normal(ks[4], (DEPTH, D_MODEL, D_MODEL), f32) * (D_MODEL ** -0.5 * BETA)
    ln1_g = 1.0 + 0.02 * jax.random.normal(ks[5], (DEPTH, D_MODEL), f32)
    ln1_b = 0.02 * jax.random.normal(ks[6], (DEPTH, D_MODEL), f32)
    w_gate = jax.random.normal(ks[7], (DEPTH, D_MODEL, D_FF), f32) * D_MODEL ** -0.5
    w_up = jax.random.normal(ks[8], (DEPTH, D_MODEL, D_FF), f32) * D_MODEL ** -0.5
    w_down = jax.random.normal(ks[9], (DEPTH, D_FF, D_MODEL), f32) * (D_FF ** -0.5 * BETA)
    ln2_g = 1.0 + 0.02 * jax.random.normal(ks[10], (DEPTH, D_MODEL), f32)
    ln2_b = 0.02 * jax.random.normal(ks[11], (DEPTH, D_MODEL), f32)
    return {"x": x, "w_in": w_in, "conv_w": conv_w, "attn_sinks": attn_sinks, "w_out": w_out,
            "ln1_g": ln1_g, "ln1_b": ln1_b, "w_gate": w_gate, "w_up": w_up, "w_down": w_down,
            "ln2_g": ln2_g, "ln2_b": ln2_b}


def reference(x, w_in, conv_w, attn_sinks, w_out, ln1_g, ln1_b, w_gate, w_up, w_down, ln2_g, ln2_b):
    bsz, seq = x.shape[0], x.shape[1]
    for l in range(DEPTH):
        proj = jnp.einsum('bsd,de->bse', x, w_in[l])
        q, k, v, cb, cc, ch, g_conv, g_attn = jnp.split(proj, SPLIT_POINTS, axis=-1)
        y_conv = short_conv_mixer(cb, cc, ch, conv_w[l])
        y_attn = sliding_window_attention(q, k, v, attn_sinks[l])
        merged = jax.nn.sigmoid(g_conv) * y_conv + jax.nn.sigmoid(g_attn) * y_attn
        mix = jnp.einsum('bse,ed->bsd', merged, w_out[l])
        x = layer_norm(ALPHA * x + mix, ln1_g[l], ln1_b[l])
        hid = jax.nn.silu(jnp.einsum('bsd,df->bsf', x, w_gate[l])) * jnp.einsum('bsd,df->bsf', x, w_up[l])
        ffn = jnp.einsum('bsf,fd->bsd', hid, w_down[l])
        x = layer_norm(ALPHA * x + ffn, ln2_g[l], ln2_b[l])
    return x.reshape(bsz, seq, D_MODEL)
```

```python
import functools

import jax
import jax.numpy as jnp
from jax import lax
from jax.experimental import pallas as pl
from jax.experimental.pallas import tpu as pltpu

D_MODEL = 1024
HEAD_DIM = 64
N_Q_HEADS = D_MODEL // HEAD_DIM
N_KV_HEADS = N_Q_HEADS // 4
GQA_GROUP = N_Q_HEADS // N_KV_HEADS
KV_WIDTH = N_KV_HEADS * HEAD_DIM
WINDOW = 128
BLOCK = 128
CONV_K = 3
D_FF = 2816
DEPTH = 1
ALPHA = (2.0 * DEPTH) ** 0.25
LN_EPS = 1e-5
NEG_INF = -1e30

OFF_Q = 0
OFF_K = OFF_Q + D_MODEL
OFF_V = OFF_K + KV_WIDTH
OFF_CB = OFF_V + KV_WIDTH
OFF_CC = OFF_CB + D_MODEL
OFF_CH = OFF_CC + D_MODEL
OFF_GC = OFF_CH + D_MODEL
OFF_GA = OFF_GC + D_MODEL
IN_WIDTH = OFF_GA + D_MODEL

LANES = 128
SUBLANES = 8
MIX_TILE = 512
FFN_TILE = 512
CONV_CHUNK = 256
FFN_CHUNK = 256
VMEM_LIMIT = 56 * 1024 * 1024

F32 = jnp.float32
BF16 = jnp.bfloat16


def _alibi_slope(head):
    return 2.0 ** (-8.0 * (head + 1) / N_Q_HEADS)


def _dot(a, b):
    return jnp.dot(a, b, preferred_element_type=F32)


def _dot_nt(a, b):
    return lax.dot_general(a, b, (((1,), (1,)), ((), ())), preferred_element_type=F32)


def _layer_norm(z, g, b):
    mu = jnp.mean(z, axis=-1, keepdims=True)
    d = z - mu
    var = jnp.mean(d * d, axis=-1, keepdims=True)
    return d * lax.rsqrt(var + LN_EPS) * g + b


def _mixer_kernel(sinks_ref, x_ref, w_in_ref, conv_w_ref, w_out_ref, g_ref, b_ref, o_ref,
                  q_scr, k_scr, vv_scr, ucarry_scr, merged_scr, ga_scr):
    tm = x_ref.shape[0]
    nblk = tm // BLOCK
    i = pl.program_id(1)

    @pl.when(i == 0)
    def _():
        k_scr[0:BLOCK, :] = jnp.zeros((BLOCK, k_scr.shape[1]), BF16)
        vv_scr[0:BLOCK, :] = jnp.zeros((BLOCK, vv_scr.shape[1]), BF16)
        ucarry_scr[...] = jnp.zeros(ucarry_scr.shape, F32)

    @pl.when(i > 0)
    def _():
        k_scr[0:BLOCK, :] = k_scr[tm:tm + BLOCK, :]
        vv_scr[0:BLOCK, :] = vv_scr[tm:tm + BLOCK, :]

    xb = x_ref[...].astype(BF16)
    lane = lax.broadcasted_iota(jnp.int32, (tm, LANES), 1)
    lo = lane < HEAD_DIM

    qkv = _dot(xb, w_in_ref[:, OFF_Q:OFF_CB])
    for blk in range(nblk):
        for pr in range(N_Q_HEADS // 2):
            qp = qkv[blk * BLOCK:(blk + 1) * BLOCK, pr * LANES:(pr + 1) * LANES] * (HEAD_DIM ** -0.5)
            q_scr[blk, pr // 2, (pr % 2) * BLOCK:(pr % 2 + 1) * BLOCK, :] = qp.astype(BF16)

    for t in range(KV_WIDTH // LANES):
        kt = qkv[:, OFF_K + t * LANES:OFF_K + (t + 1) * LANES]
        kr = pltpu.roll(kt, HEAD_DIM, axis=1)
        vt = qkv[:, OFF_V + t * LANES:OFF_V + (t + 1) * LANES]
        vr = pltpu.roll(vt, HEAD_DIM, axis=1)
        zero = jnp.zeros_like(kt)
        h0, h1 = 2 * t, 2 * t + 1
        k_scr[BLOCK:, (2 * h0) * LANES:(2 * h0 + 1) * LANES] = jnp.where(lo, kt, zero).astype(BF16)
        k_scr[BLOCK:, (2 * h0 + 1) * LANES:(2 * h0 + 2) * LANES] = jnp.where(lo, zero, kr).astype(BF16)
        k_scr[BLOCK:, (2 * h1) * LANES:(2 * h1 + 1) * LANES] = jnp.where(lo, kr, zero).astype(BF16)
        k_scr[BLOCK:, (2 * h1 + 1) * LANES:(2 * h1 + 2) * LANES] = jnp.where(lo, zero, kt).astype(BF16)
        vv_scr[BLOCK:, h0 * LANES:(h0 + 1) * LANES] = jnp.where(lo, vt, vr).astype(BF16)
        vv_scr[BLOCK:, h1 * LANES:(h1 + 1) * LANES] = jnp.where(lo, vr, vt).astype(BF16)

    row = lax.broadcasted_iota(jnp.int32, (tm, CONV_CHUNK), 0)
    for c in range(D_MODEL // CONV_CHUNK):
        c0 = c * CONV_CHUNK
        cb = _dot(xb, w_in_ref[:, OFF_CB + c0:OFF_CB + c0 + CONV_CHUNK])
        cc = _dot(xb, w_in_ref[:, OFF_CC + c0:OFF_CC + c0 + CONV_CHUNK])
        ch = _dot(xb, w_in_ref[:, OFF_CH + c0:OFF_CH + c0 + CONV_CHUNK])
        gc = _dot(xb, w_in_ref[:, OFF_GC + c0:OFF_GC + c0 + CONV_CHUNK])
        u = cc * ch
        last = ucarry_scr[SUBLANES - 1:SUBLANES, c0:c0 + CONV_CHUNK]
        prev = ucarry_scr[SUBLANES - 2:SUBLANES - 1, c0:c0 + CONV_CHUNK]
        u1 = jnp.where(row == 0, last, pltpu.roll(u, 1, axis=0))
        u2 = jnp.where(row == 0, prev, jnp.where(row == 1, last, pltpu.roll(u, 2, axis=0)))
        w = conv_w_ref[:, c0:c0 + CONV_CHUNK]
        y = w[0:1, :] * u2 + w[1:2, :] * u1 + w[2:3, :] * u
        merged_scr[:, c0:c0 + CONV_CHUNK] = jax.nn.sigmoid(gc) * (cb * y)
        ucarry_scr[:, c0:c0 + CONV_CHUNK] = u[tm - SUBLANES:tm, :]

    ga_scr[...] = jax.nn.sigmoid(_dot(xb, w_in_ref[:, OFF_GA:OFF_GA + D_MODEL]))

    qi = lax.broadcasted_iota(jnp.int32, (BLOCK, BLOCK), 0)
    kj = lax.broadcasted_iota(jnp.int32, (BLOCK, BLOCK), 1)
    tri = kj <= qi
    dist = jnp.where(tri, qi - kj, qi - kj + BLOCK).astype(F32)
    lo_b = kj < HEAD_DIM

    def block_body(j, carry):
        r0 = pl.multiple_of(j * BLOCK, BLOCK)
        first = jnp.logical_and(i == 0, j == 0)
        valid = (kj - qi) <= jnp.where(first, 0, BLOCK)
        for h in range(N_KV_HEADS):
            qg = q_scr[j, h]
            ke = k_scr[pl.ds(r0, 2 * BLOCK), (2 * h) * LANES:(2 * h + 1) * LANES]
            ko = k_scr[pl.ds(r0, 2 * BLOCK), (2 * h + 1) * LANES:(2 * h + 2) * LANES]
            se = _dot_nt(qg, ke)
            so = _dot_nt(qg, ko)
            probs, inv = [], []
            for g, (s, half) in enumerate(((se, 0), (so, 0), (se, 1), (so, 1))):
                head = GQA_GROUP * h + g
                s_prev = s[half * BLOCK:(half + 1) * BLOCK, 0:BLOCK]
                s_cur = s[half * BLOCK:(half + 1) * BLOCK, BLOCK:2 * BLOCK]
                sm = jnp.where(tri, s_cur, s_prev) + (-_alibi_slope(head)) * dist
                sm = jnp.where(valid, sm, NEG_INF)
                sink = sinks_ref[head]
                m = jnp.maximum(jnp.max(sm, axis=-1, keepdims=True), sink)
                p = jnp.exp(sm - m)
                denom = jnp.sum(p, axis=-1, keepdims=True) + jnp.exp(sink - m)
                inv.append(1.0 / denom)
                zero = jnp.zeros_like(p)
                probs.append(jnp.concatenate([jnp.where(tri, zero, p), jnp.where(tri, p, zero)],
                                             axis=1).astype(BF16))
            pst = jnp.concatenate(probs, axis=0)
            vv = vv_scr[pl.ds(r0, 2 * BLOCK), h * LANES:(h + 1) * LANES]
            o = _dot(pst, vv)
            for pr in range(2):
                oe = o[(2 * pr) * BLOCK:(2 * pr + 1) * BLOCK, :] * inv[2 * pr]
                oo = o[(2 * pr + 1) * BLOCK:(2 * pr + 2) * BLOCK, :] * inv[2 * pr + 1]
                col = (2 * h + pr) * LANES
                gate = ga_scr[pl.ds(r0, BLOCK), col:col + LANES]
                merged_scr[pl.ds(r0, BLOCK), col:col + LANES] += gate * jnp.where(lo_b, oe, oo)
        return carry

    lax.fori_loop(0, nblk, block_body, 0)

    mix = _dot(merged_scr[...].astype(BF16), w_out_ref[...])
    z = ALPHA * x_ref[...] + mix
    o_ref[...] = _layer_norm(z, g_ref[...], b_ref[...])


def _ffn_kernel(x_ref, w_gate_ref, w_up_ref, w_down_ref, g_ref, b_ref, o_ref, hid_scr):
    xb = x_ref[...].astype(BF16)
    for c in range(D_FF // FFN_CHUNK):
        c0 = c * FFN_CHUNK
        gate = _dot(xb, w_gate_ref[:, c0:c0 + FFN_CHUNK])
        up = _dot(xb, w_up_ref[:, c0:c0 + FFN_CHUNK])
        hid_scr[:, c0:c0 + FFN_CHUNK] = (gate * jax.nn.sigmoid(gate) * up).astype(BF16)
    ffn = _dot(hid_scr[...], w_down_ref[...])
    z = ALPHA * x_ref[...] + ffn
    o_ref[...] = _layer_norm(z, g_ref[...], b_ref[...])


def _resident(shape):
    return pl.BlockSpec(shape, lambda *_: (0,) * len(shape), pipeline_mode=pl.Buffered(1))


def _mixer_call(x, w_in, conv_w, sinks, w_out, g, b):
    bsz, seq, _ = x.shape
    tm = MIX_TILE
    nblk = tm // BLOCK
    tile = pl.BlockSpec((None, tm, D_MODEL), lambda bi, si: (bi, si, 0))
    return pl.pallas_call(
        _mixer_kernel,
        grid=(bsz, seq // tm),
        in_specs=[
            pl.BlockSpec(memory_space=pltpu.SMEM),
            tile,
            _resident((D_MODEL, IN_WIDTH)),
            _resident((CONV_K, D_MODEL)),
            _resident((D_MODEL, D_MODEL)),
            _resident((1, D_MODEL)),
            _resident((1, D_MODEL)),
        ],
        out_specs=tile,
        out_shape=jax.ShapeDtypeStruct(x.shape, F32),
        scratch_shapes=[
            pltpu.VMEM((nblk, N_KV_HEADS, 2 * BLOCK, LANES), BF16),
            pltpu.VMEM((tm + BLOCK, 2 * N_KV_HEADS * LANES), BF16),
            pltpu.VMEM((tm + BLOCK, N_KV_HEADS * LANES), BF16),
            pltpu.VMEM((SUBLANES, D_MODEL), F32),
            pltpu.VMEM((tm, D_MODEL), F32),
            pltpu.VMEM((tm, D_MODEL), F32),
        ],
        compiler_params=pltpu.CompilerParams(
            dimension_semantics=("arbitrary", "arbitrary"), vmem_limit_bytes=VMEM_LIMIT),
        name="mixer",
    )(sinks, x, w_in, conv_w, w_out, g, b)


def _ffn_call(x, w_gate, w_up, w_down, g, b):
    bsz, seq, _ = x.shape
    tm = FFN_TILE
    tile = pl.BlockSpec((None, tm, D_MODEL), lambda bi, si: (bi, si, 0))
    return pl.pallas_call(
        _ffn_kernel,
        grid=(bsz, seq // tm),
        in_specs=[
            tile,
            _resident((D_MODEL, D_FF)),
            _resident((D_MODEL, D_FF)),
            _resident((D_FF, D_MODEL)),
            _resident((1, D_MODEL)),
            _resident((1, D_MODEL)),
        ],
        out_specs=tile,
        out_shape=jax.ShapeDtypeStruct(x.shape, F32),
        scratch_shapes=[pltpu.VMEM((tm, D_FF), BF16)],
        compiler_params=pltpu.CompilerParams(
            dimension_semantics=("arbitrary", "arbitrary"), vmem_limit_bytes=VMEM_LIMIT),
        name="ffn",
    )(x, w_gate, w_up, w_down, g, b)


def kernel(x, w_in, conv_w, attn_sinks, w_out, ln1_g, ln1_b, w_gate, w_up, w_down, ln2_g, ln2_b):
    for l in range(DEPTH):
        x = _mixer_call(x, w_in[l].astype(BF16), conv_w[l], attn_sinks[l], w_out[l].astype(BF16),
                        ln1_g[l][None, :], ln1_b[l][None, :])
        x = _ffn_call(x, w_gate[l].astype(BF16), w_up[l].astype(BF16), w_down[l].astype(BF16),
                      ln2_g[l][None, :], ln2_b[l][None, :])
    return x
```

```python
import functools

import numpy as np

import jax
import jax.numpy as jnp
from jax import lax
from jax.experimental import pallas as pl
from jax.experimental.pallas import tpu as pltpu

D_MODEL = 1024
HEAD_DIM = 64
N_Q_HEADS = D_MODEL // HEAD_DIM
N_KV_HEADS = N_Q_HEADS // 4
GQA_GROUP = N_Q_HEADS // N_KV_HEADS
KV_WIDTH = N_KV_HEADS * HEAD_DIM
WINDOW = 128
BLOCK = 128
CONV_K = 3
D_FF = 2816
DEPTH = 1
ALPHA = (2.0 * DEPTH) ** 0.25
LN_EPS = 1e-5
NEG_INF = -1e30

OFF_Q = 0
OFF_K = OFF_Q + D_MODEL
OFF_V = OFF_K + KV_WIDTH
OFF_CB = OFF_V + KV_WIDTH
OFF_CC = OFF_CB + D_MODEL
OFF_CH = OFF_CC + D_MODEL
OFF_GC = OFF_CH + D_MODEL
OFF_GA = OFF_GC + D_MODEL
IN_WIDTH = OFF_GA + D_MODEL

LANES = 128
SUBLANES = 8
MIX_TILE = 512
FFN_TILE = 512
CONV_CHUNK = GQA_GROUP * HEAD_DIM
FFN_CHUNK = 256
VMEM_LIMIT = 56 * 1024 * 1024

F32 = jnp.float32
BF16 = jnp.bfloat16


def _alibi_bias():
    slopes = np.array([2.0 ** (-8.0 * (h + 1) / N_Q_HEADS) for h in range(N_Q_HEADS)], dtype=np.float32)
    qi = np.arange(BLOCK)[:, None]
    kj = np.arange(BLOCK)[None, :]
    dist = ((qi - kj) % BLOCK).astype(np.float32)
    return -slopes[:, None, None] * dist[None]


def _dot(a, b):
    return jnp.dot(a, b, preferred_element_type=F32)


def _dot_nt(a, b):
    return lax.dot_general(a, b, (((1,), (1,)), ((), ())), preferred_element_type=F32)


def _layer_norm(z, g, b):
    mu = jnp.mean(z, axis=-1, keepdims=True)
    d = z - mu
    var = jnp.mean(d * d, axis=-1, keepdims=True)
    return d * lax.rsqrt(var + LN_EPS) * g + b


def _deferred_norm_steps(compute_z, z_scr, g_ref, b_ref, o_ref):
    s = pl.program_id(0)
    last = pl.num_programs(0) - 1

    @pl.when(s == 0)
    def _():
        z_scr[...] = jnp.zeros(z_scr.shape, F32)

    @pl.when(s < last)
    def _():
        o_ref[...] = _layer_norm(z_scr[...], g_ref[...], b_ref[...])
        z_scr[...] = compute_z()

    @pl.when(s == last)
    def _():
        o_ref[...] = _layer_norm(z_scr[...], g_ref[...], b_ref[...])


def _mixer_kernel(sinks_ref, x_ref, w_in_ref, conv_w_ref, w_out_ref, g_ref, b_ref, bias_ref, o_ref,
                  q_scr, k_scr, vv_scr, ucarry_scr, merged_scr, z_scr, *, tiles_per_seq):
    _deferred_norm_steps(
        lambda: _mixer_tile(sinks_ref, x_ref, w_in_ref, conv_w_ref, w_out_ref, bias_ref,
                            q_scr, k_scr, vv_scr, ucarry_scr, merged_scr, tiles_per_seq),
        z_scr, g_ref, b_ref, o_ref)


def _mixer_tile(sinks_ref, x_ref, w_in_ref, conv_w_ref, w_out_ref, bias_ref,
                q_scr, k_scr, vv_scr, ucarry_scr, merged_scr, tiles_per_seq):
    tm = x_ref.shape[0]
    nblk = tm // BLOCK
    i = lax.rem(pl.program_id(0), tiles_per_seq)

    @pl.when(i == 0)
    def _():
        k_scr[0:BLOCK, :] = jnp.zeros((BLOCK, k_scr.shape[1]), BF16)
        vv_scr[0:BLOCK, :] = jnp.zeros((BLOCK, vv_scr.shape[1]), BF16)
        ucarry_scr[...] = jnp.zeros(ucarry_scr.shape, F32)

    @pl.when(i > 0)
    def _():
        k_scr[0:BLOCK, :] = k_scr[tm:tm + BLOCK, :]
        vv_scr[0:BLOCK, :] = vv_scr[tm:tm + BLOCK, :]

    xb = x_ref[...].astype(BF16)
    lane = lax.broadcasted_iota(jnp.int32, (tm, LANES), 1)
    lo = lane < HEAD_DIM

    qkv = _dot(xb, w_in_ref[:, OFF_Q:OFF_CB])
    for blk in range(nblk):
        for pr in range(N_Q_HEADS // 2):
            qp = qkv[blk * BLOCK:(blk + 1) * BLOCK, pr * LANES:(pr + 1) * LANES] * (HEAD_DIM ** -0.5)
            q_scr[blk, pr // 2, (pr % 2) * BLOCK:(pr % 2 + 1) * BLOCK, :] = qp.astype(BF16)

    for t in range(KV_WIDTH // LANES):
        kt = qkv[:, OFF_K + t * LANES:OFF_K + (t + 1) * LANES]
        kr = pltpu.roll(kt, HEAD_DIM, axis=1)
        vt = qkv[:, OFF_V + t * LANES:OFF_V + (t + 1) * LANES]
        vr = pltpu.roll(vt, HEAD_DIM, axis=1)
        zero = jnp.zeros_like(kt)
        h0, h1 = 2 * t, 2 * t + 1
        k_scr[BLOCK:, (2 * h0) * LANES:(2 * h0 + 1) * LANES] = jnp.where(lo, kt, zero).astype(BF16)
        k_scr[BLOCK:, (2 * h0 + 1) * LANES:(2 * h0 + 2) * LANES] = jnp.where(lo, zero, kr).astype(BF16)
        k_scr[BLOCK:, (2 * h1) * LANES:(2 * h1 + 1) * LANES] = jnp.where(lo, kr, zero).astype(BF16)
        k_scr[BLOCK:, (2 * h1 + 1) * LANES:(2 * h1 + 2) * LANES] = jnp.where(lo, zero, kt).astype(BF16)
        vv_scr[BLOCK:, h0 * LANES:(h0 + 1) * LANES] = jnp.where(lo, vt, vr).astype(BF16)
        vv_scr[BLOCK:, h1 * LANES:(h1 + 1) * LANES] = jnp.where(lo, vr, vt).astype(BF16)

    qi = lax.broadcasted_iota(jnp.int32, (BLOCK, BLOCK), 0)
    kj = lax.broadcasted_iota(jnp.int32, (BLOCK, BLOCK), 1)
    tri = kj <= qi
    lo_b = kj < HEAD_DIM
    valid0 = (kj - qi) <= jnp.where(i == 0, 0, BLOCK)
    row = lax.broadcasted_iota(jnp.int32, (tm, CONV_CHUNK), 0)

    def attend(blk, h):
        r0 = blk * BLOCK
        qg = q_scr[blk, h]
        ke = k_scr[r0:r0 + 2 * BLOCK, (2 * h) * LANES:(2 * h + 1) * LANES]
        ko = k_scr[r0:r0 + 2 * BLOCK, (2 * h + 1) * LANES:(2 * h + 2) * LANES]
        se = _dot_nt(qg, ke)
        so = _dot_nt(qg, ko)
        probs, inv = [], []
        for g, (s, half) in enumerate(((se, 0), (so, 0), (se, 1), (so, 1))):
            head = GQA_GROUP * h + g
            s_prev = s[half * BLOCK:(half + 1) * BLOCK, 0:BLOCK]
            s_cur = s[half * BLOCK:(half + 1) * BLOCK, BLOCK:2 * BLOCK]
            sm = jnp.where(tri, s_cur, s_prev) + bias_ref[head]
            if blk == 0:
                sm = jnp.where(valid0, sm, NEG_INF)
            sink = sinks_ref[head]
            m = jnp.maximum(jnp.max(sm, axis=-1, keepdims=True), sink)
            p = jnp.exp(sm - m)
            denom = jnp.sum(p, axis=-1, keepdims=True) + jnp.exp(sink - m)
            inv.append(1.0 / denom)
            zero = jnp.zeros_like(p)
            probs.append(jnp.concatenate([jnp.where(tri, zero, p), jnp.where(tri, p, zero)],
                                         axis=1).astype(BF16))
        pst = jnp.concatenate(probs, axis=0)
        vv = vv_scr[r0:r0 + 2 * BLOCK, h * LANES:(h + 1) * LANES]
        o = _dot(pst, vv)
        pairs = []
        for pr in range(2):
            oe = o[(2 * pr) * BLOCK:(2 * pr + 1) * BLOCK, :] * inv[2 * pr]
            oo = o[(2 * pr + 1) * BLOCK:(2 * pr + 2) * BLOCK, :] * inv[2 * pr + 1]
            pairs.append(jnp.where(lo_b, oe, oo))
        return jnp.concatenate(pairs, axis=1)

    for h in range(N_KV_HEADS):
        c0 = h * CONV_CHUNK
        cb = _dot(xb, w_in_ref[:, OFF_CB + c0:OFF_CB + c0 + CONV_CHUNK])
        cc = _dot(xb, w_in_ref[:, OFF_CC + c0:OFF_CC + c0 + CONV_CHUNK])
        ch = _dot(xb, w_in_ref[:, OFF_CH + c0:OFF_CH + c0 + CONV_CHUNK])
        gc = _dot(xb, w_in_ref[:, OFF_GC + c0:OFF_GC + c0 + CONV_CHUNK])
        ga = _dot(xb, w_in_ref[:, OFF_GA + c0:OFF_GA + c0 + CONV_CHUNK])
        u = cc * ch
        last = ucarry_scr[SUBLANES - 1:SUBLANES, c0:c0 + CONV_CHUNK]
        prev = ucarry_scr[SUBLANES - 2:SUBLANES - 1, c0:c0 + CONV_CHUNK]
        u1 = jnp.where(row == 0, last, pltpu.roll(u, 1, axis=0))
        u2 = jnp.where(row == 0, prev, jnp.where(row == 1, last, pltpu.roll(u, 2, axis=0)))
        w = conv_w_ref[:, c0:c0 + CONV_CHUNK]
        y = w[0:1, :] * u2 + w[1:2, :] * u1 + w[2:3, :] * u
        ucarry_scr[:, c0:c0 + CONV_CHUNK] = u[tm - SUBLANES:tm, :]
        y_attn = jnp.concatenate([attend(blk, h) for blk in range(nblk)], axis=0)
        merged = jax.nn.sigmoid(gc) * (cb * y) + jax.nn.sigmoid(ga) * y_attn
        merged_scr[:, c0:c0 + CONV_CHUNK] = merged.astype(BF16)

    mix = _dot(merged_scr[...], w_out_ref[...])
    return ALPHA * x_ref[...] + mix


def _ffn_kernel(x_ref, w_gate_ref, w_up_ref, w_down_ref, g_ref, b_ref, o_ref, hid_scr, z_scr):
    def ffn_tile():
        xb = x_ref[...].astype(BF16)
        for c in range(D_FF // FFN_CHUNK):
            c0 = c * FFN_CHUNK
            gate = _dot(xb, w_gate_ref[:, c0:c0 + FFN_CHUNK])
            up = _dot(xb, w_up_ref[:, c0:c0 + FFN_CHUNK])
            hid_scr[:, c0:c0 + FFN_CHUNK] = (gate * jax.nn.sigmoid(gate) * up).astype(BF16)
        ffn = _dot(hid_scr[...], w_down_ref[...])
        return ALPHA * x_ref[...] + ffn

    _deferred_norm_steps(ffn_tile, z_scr, g_ref, b_ref, o_ref)


def _resident(shape):
    return pl.BlockSpec(shape, lambda *_: (0,) * len(shape), pipeline_mode=pl.Buffered(1))


def _lagged_tile_specs(tm, n_tiles):
    in_tile = pl.BlockSpec((tm, D_MODEL), lambda s: (jnp.minimum(s, n_tiles - 1), 0))
    out_tile = pl.BlockSpec((tm, D_MODEL), lambda s: (jnp.maximum(s - 1, 0), 0))
    return in_tile, out_tile


def _mixer_call(x, w_in, conv_w, sinks, w_out, g, b, seq):
    tokens = x.shape[0]
    tm = MIX_TILE
    nblk = tm // BLOCK
    n_tiles = tokens // tm
    in_tile, out_tile = _lagged_tile_specs(tm, n_tiles)
    return pl.pallas_call(
        functools.partial(_mixer_kernel, tiles_per_seq=seq // tm),
        grid=(n_tiles + 1,),
        in_specs=[
            pl.BlockSpec(memory_space=pltpu.SMEM),
            in_tile,
            _resident((D_MODEL, IN_WIDTH)),
            _resident((CONV_K, D_MODEL)),
            _resident((D_MODEL, D_MODEL)),
            _resident((1, D_MODEL)),
            _resident((1, D_MODEL)),
            _resident((N_Q_HEADS, BLOCK, BLOCK)),
        ],
        out_specs=out_tile,
        out_shape=jax.ShapeDtypeStruct(x.shape, F32),
        scratch_shapes=[
            pltpu.VMEM((nblk, N_KV_HEADS, 2 * BLOCK, LANES), BF16),
            pltpu.VMEM((tm + BLOCK, 2 * N_KV_HEADS * LANES), BF16),
            pltpu.VMEM((tm + BLOCK, N_KV_HEADS * LANES), BF16),
            pltpu.VMEM((SUBLANES, D_MODEL), F32),
            pltpu.VMEM((tm, D_MODEL), BF16),
            pltpu.VMEM((tm, D_MODEL), F32),
        ],
        compiler_params=pltpu.CompilerParams(
            dimension_semantics=("arbitrary",), vmem_limit_bytes=VMEM_LIMIT),
        name="mixer",
    )(sinks, x, w_in, conv_w, w_out, g, b, jnp.asarray(_alibi_bias()))


def _ffn_call(x, w_gate, w_up, w_down, g, b):
    tokens = x.shape[0]
    tm = FFN_TILE
    n_tiles = tokens // tm
    in_tile, out_tile = _lagged_tile_specs(tm, n_tiles)
    return pl.pallas_call(
        _ffn_kernel,
        grid=(n_tiles + 1,),
        in_specs=[
            in_tile,
            _resident((D_MODEL, D_FF)),
            _resident((D_MODEL, D_FF)),
            _resident((D_FF, D_MODEL)),
            _resident((1, D_MODEL)),
            _resident((1, D_MODEL)),
        ],
        out_specs=out_tile,
        out_shape=jax.ShapeDtypeStruct(x.shape, F32),
        scratch_shapes=[
            pltpu.VMEM((tm, D_FF), BF16),
            pltpu.VMEM((tm, D_MODEL), F32),
        ],
        compiler_params=pltpu.CompilerParams(
            dimension_semantics=("arbitrary",), vmem_limit_bytes=VMEM_LIMIT),
        name="ffn",
    )(x, w_gate, w_up, w_down, g, b)


def kernel(x, w_in, conv_w, attn_sinks, w_out, ln1_g, ln1_b, w_gate, w_up, w_down, ln2_g, ln2_b):
    bsz, seq, _ = x.shape
    assert seq % MIX_TILE == 0 and (bsz * seq) % FFN_TILE == 0
    x = x.reshape(bsz * seq, D_MODEL)
    for l in range(DEPTH):
        x = _mixer_call(x, w_in[l].astype(BF16), conv_w[l], attn_sinks[l], w_out[l].astype(BF16),
                        ln1_g[l][None, :], ln1_b[l][None, :], seq)
        x = _ffn_call(x, w_gate[l].astype(BF16), w_up[l].astype(BF16), w_down[l].astype(BF16),
                      ln2_g[l][None, :], ln2_b[l][None, :])
    return x.reshape(bsz, seq, D_MODEL)
```

```python
import functools

import numpy as np

import jax
import jax.numpy as jnp
from jax import lax
from jax.experimental import pallas as pl
from jax.experimental.pallas import tpu as pltpu

D_MODEL = 1024
HEAD_DIM = 64
N_Q_HEADS = D_MODEL // HEAD_DIM
N_KV_HEADS = N_Q_HEADS // 4
GQA_GROUP = N_Q_HEADS // N_KV_HEADS
KV_WIDTH = N_KV_HEADS * HEAD_DIM
WINDOW = 128
BLOCK = 128
CONV_K = 3
D_FF = 2816
DEPTH = 1
ALPHA = (2.0 * DEPTH) ** 0.25
LN_EPS = 1e-5
NEG_INF = -1e30

OFF_Q = 0
OFF_K = OFF_Q + D_MODEL
OFF_V = OFF_K + KV_WIDTH
OFF_CB = OFF_V + KV_WIDTH
OFF_CC = OFF_CB + D_MODEL
OFF_CH = OFF_CC + D_MODEL
OFF_GC = OFF_CH + D_MODEL
OFF_GA = OFF_GC + D_MODEL
IN_WIDTH = OFF_GA + D_MODEL

LANES = 128
SUBLANES = 8
TILE = 256
CONV_CHUNK = GQA_GROUP * HEAD_DIM
FFN_CHUNK = 256
N_FFN_CHUNKS = D_FF // FFN_CHUNK
FFN_HEAD_CHUNKS = N_FFN_CHUNKS - 2 * N_KV_HEADS
VMEM_LIMIT = 56 * 1024 * 1024

F32 = jnp.float32
BF16 = jnp.bfloat16


def _alibi_bias():
    slopes = np.array([2.0 ** (-8.0 * (h + 1) / N_Q_HEADS) for h in range(N_Q_HEADS)], dtype=np.float32)
    qi = np.arange(BLOCK)[:, None]
    kj = np.arange(BLOCK)[None, :]
    dist = ((qi - kj) % BLOCK).astype(np.float32)
    return -slopes[:, None, None] * dist[None]


def _dot(a, b):
    return jnp.dot(a, b, preferred_element_type=F32)


def _dot_nt(a, b):
    return lax.dot_general(a, b, (((1,), (1,)), ((), ())), preferred_element_type=F32)


def _layer_norm(z, g, b):
    mu = jnp.mean(z, axis=-1, keepdims=True)
    d = z - mu
    var = jnp.mean(d * d, axis=-1, keepdims=True)
    return d * lax.rsqrt(var + LN_EPS) * g + b


def _block_kernel(sinks_ref, x_ref, w_in_ref, conv_w_ref, w_out_ref, g1_ref, b1_ref, bias_ref,
                  w_gate_ref, w_up_ref, w_down_ref, g2_ref, b2_ref, o_ref,
                  q_scr, k_scr, vv_scr, ucarry_scr, merged_scr, z1_scr, hid_scr, *, tiles_per_seq):
    tm = x_ref.shape[0]
    nblk = tm // BLOCK
    s = pl.program_id(0)
    i = lax.rem(s, tiles_per_seq)
    seq_start = i == 0

    @pl.when(s == 0)
    def _():
        z1_scr[...] = jnp.zeros(z1_scr.shape, F32)
        k_scr[...] = jnp.zeros(k_scr.shape, BF16)
        vv_scr[...] = jnp.zeros(vv_scr.shape, BF16)
        ucarry_scr[...] = jnp.zeros(ucarry_scr.shape, F32)

    x1 = _layer_norm(z1_scr[...], g1_ref[...], b1_ref[...])
    x1b = x1.astype(BF16)

    def gate_up(c):
        c0 = c * FFN_CHUNK
        gate = _dot(x1b, w_gate_ref[:, c0:c0 + FFN_CHUNK])
        up = _dot(x1b, w_up_ref[:, c0:c0 + FFN_CHUNK])
        hid_scr[:, c0:c0 + FFN_CHUNK] = (gate * jax.nn.sigmoid(gate) * up).astype(BF16)

    k_scr[0:BLOCK, :] = jnp.where(seq_start, jnp.zeros((BLOCK, k_scr.shape[1]), BF16), k_scr[tm:tm + BLOCK, :])
    vv_scr[0:BLOCK, :] = jnp.where(seq_start, jnp.zeros((BLOCK, vv_scr.shape[1]), BF16), vv_scr[tm:tm + BLOCK, :])

    xb = x_ref[...].astype(BF16)
    lane = lax.broadcasted_iota(jnp.int32, (tm, LANES), 1)
    lo = lane < HEAD_DIM

    qkv = _dot(xb, w_in_ref[:, OFF_Q:OFF_CB])
    for blk in range(nblk):
        for pr in range(N_Q_HEADS // 2):
            qp = qkv[blk * BLOCK:(blk + 1) * BLOCK, pr * LANES:(pr + 1) * LANES] * (HEAD_DIM ** -0.5)
            q_scr[blk, pr // 2, (pr % 2) * BLOCK:(pr % 2 + 1) * BLOCK, :] = qp.astype(BF16)

    for t in range(KV_WIDTH // LANES):
        kt = qkv[:, OFF_K + t * LANES:OFF_K + (t + 1) * LANES]
        kr = pltpu.roll(kt, HEAD_DIM, axis=1)
        vt = qkv[:, OFF_V + t * LANES:OFF_V + (t + 1) * LANES]
        vr = pltpu.roll(vt, HEAD_DIM, axis=1)
        zero = jnp.zeros_like(kt)
        h0, h1 = 2 * t, 2 * t + 1
        k_scr[BLOCK:, (2 * h0) * LANES:(2 * h0 + 1) * LANES] = jnp.where(lo, kt, zero).astype(BF16)
        k_scr[BLOCK:, (2 * h0 + 1) * LANES:(2 * h0 + 2) * LANES] = jnp.where(lo, zero, kr).astype(BF16)
        k_scr[BLOCK:, (2 * h1) * LANES:(2 * h1 + 1) * LANES] = jnp.where(lo, kr, zero).astype(BF16)
        k_scr[BLOCK:, (2 * h1 + 1) * LANES:(2 * h1 + 2) * LANES] = jnp.where(lo, zero, kt).astype(BF16)
        vv_scr[BLOCK:, h0 * LANES:(h0 + 1) * LANES] = jnp.where(lo, vt, vr).astype(BF16)
        vv_scr[BLOCK:, h1 * LANES:(h1 + 1) * LANES] = jnp.where(lo, vr, vt).astype(BF16)

    for c in range(FFN_HEAD_CHUNKS):
        gate_up(c)

    qi = lax.broadcasted_iota(jnp.int32, (BLOCK, BLOCK), 0)
    kj = lax.broadcasted_iota(jnp.int32, (BLOCK, BLOCK), 1)
    tri = kj <= qi
    lo_b = kj < HEAD_DIM
    valid0 = (kj - qi) <= jnp.where(seq_start, 0, BLOCK)
    row = lax.broadcasted_iota(jnp.int32, (tm, CONV_CHUNK), 0)

    def scores(blk, h):
        r0 = blk * BLOCK
        qg = q_scr[blk, h]
        ke = k_scr[r0:r0 + 2 * BLOCK, (2 * h) * LANES:(2 * h + 1) * LANES]
        ko = k_scr[r0:r0 + 2 * BLOCK, (2 * h + 1) * LANES:(2 * h + 2) * LANES]
        se = _dot_nt(qg, ke)
        so = _dot_nt(qg, ko)
        return se, so

    def attend(blk, h, se, so):
        r0 = blk * BLOCK
        probs, inv = [], []
        for g, (sc, half) in enumerate(((se, 0), (so, 0), (se, 1), (so, 1))):
            head = GQA_GROUP * h + g
            s_prev = sc[half * BLOCK:(half + 1) * BLOCK, 0:BLOCK]
            s_cur = sc[half * BLOCK:(half + 1) * BLOCK, BLOCK:2 * BLOCK]
            sm = jnp.where(tri, s_cur, s_prev) + bias_ref[head]
            if blk == 0:
                sm = jnp.where(valid0, sm, NEG_INF)
            sink = sinks_ref[head]
            m = jnp.maximum(jnp.max(sm, axis=-1, keepdims=True), sink)
            p = jnp.exp(sm - m)
            denom = jnp.sum(p, axis=-1, keepdims=True) + jnp.exp(sink - m)
            inv.append(1.0 / denom)
            zero = jnp.zeros_like(p)
            probs.append(jnp.concatenate([jnp.where(tri, zero, p), jnp.where(tri, p, zero)],
                                         axis=1).astype(BF16))
        pst = jnp.concatenate(probs, axis=0)
        vv = vv_scr[r0:r0 + 2 * BLOCK, h * LANES:(h + 1) * LANES]
        o = _dot(pst, vv)
        pairs = []
        for pr in range(2):
            oe = o[(2 * pr) * BLOCK:(2 * pr + 1) * BLOCK, :] * inv[2 * pr]
            oo = o[(2 * pr + 1) * BLOCK:(2 * pr + 2) * BLOCK, :] * inv[2 * pr + 1]
            pairs.append(jnp.where(lo_b, oe, oo))
        return jnp.concatenate(pairs, axis=1)

    def proj(off, h):
        c0 = off + h * CONV_CHUNK
        return _dot(xb, w_in_ref[:, c0:c0 + CONV_CHUNK])

    units = [(h, blk) for h in range(N_KV_HEADS) for blk in range(nblk)]
    assert nblk == 2
    sc = scores(0, 0)
    for n, (h, blk) in enumerate(units):
        c0 = h * CONV_CHUNK
        if blk == 0:
            cb, cc, ch = proj(OFF_CB, h), proj(OFF_CC, h), proj(OFF_CH, h)
            y_blocks = []
        else:
            gc, ga = proj(OFF_GC, h), proj(OFF_GA, h)
        gate_up(FFN_HEAD_CHUNKS + n)
        if n + 1 < len(units):
            sc_next = scores(units[n + 1][1], units[n + 1][0])
        y_blocks.append(attend(blk, h, *sc))
        sc = sc_next
        if blk == 0:
            continue
        y_attn = jnp.concatenate(y_blocks, axis=0)
        u = cc * ch
        zrow = jnp.zeros((1, CONV_CHUNK), F32)
        last = jnp.where(seq_start, zrow, ucarry_scr[SUBLANES - 1:SUBLANES, c0:c0 + CONV_CHUNK])
        prev = jnp.where(seq_start, zrow, ucarry_scr[SUBLANES - 2:SUBLANES - 1, c0:c0 + CONV_CHUNK])
        u1 = jnp.where(row == 0, last, pltpu.roll(u, 1, axis=0))
        u2 = jnp.where(row == 0, prev, jnp.where(row == 1, last, pltpu.roll(u, 2, axis=0)))
        w = conv_w_ref[:, c0:c0 + CONV_CHUNK]
        y = w[0:1, :] * u2 + w[1:2, :] * u1 + w[2:3, :] * u
        ucarry_scr[:, c0:c0 + CONV_CHUNK] = u[tm - SUBLANES:tm, :]
        merged = jax.nn.sigmoid(gc) * (cb * y) + jax.nn.sigmoid(ga) * y_attn
        merged_scr[:, c0:c0 + CONV_CHUNK] = merged.astype(BF16)

    ffn = _dot(hid_scr[...], w_down_ref[...])
    o_ref[...] = _layer_norm(ALPHA * x1 + ffn, g2_ref[...], b2_ref[...])

    mix = _dot(merged_scr[...], w_out_ref[...])
    z1_scr[...] = ALPHA * x_ref[...] + mix


def _resident(shape):
    return pl.BlockSpec(shape, lambda *_: (0,) * len(shape), pipeline_mode=pl.Buffered(1))


def _block_call(x, sinks, w_in, conv_w, w_out, g1, b1, w_gate, w_up, w_down, g2, b2, seq):
    tokens = x.shape[0]
    tm = TILE
    nblk = tm // BLOCK
    n_tiles = tokens // tm
    in_tile = pl.BlockSpec((tm, D_MODEL), lambda s: (jnp.minimum(s, n_tiles - 1), 0))
    out_tile = pl.BlockSpec((tm, D_MODEL), lambda s: (jnp.maximum(s - 1, 0), 0))
    return pl.pallas_call(
        functools.partial(_block_kernel, tiles_per_seq=seq // tm),
        grid=(n_tiles + 1,),
        in_specs=[
            pl.BlockSpec(memory_space=pltpu.SMEM),
            in_tile,
            _resident((D_MODEL, IN_WIDTH)),
            _resident((CONV_K, D_MODEL)),
            _resident((D_MODEL, D_MODEL)),
            _resident((1, D_MODEL)),
            _resident((1, D_MODEL)),
            _resident((N_Q_HEADS, BLOCK, BLOCK)),
            _resident((D_MODEL, D_FF)),
            _resident((D_MODEL, D_FF)),
            _resident((D_FF, D_MODEL)),
            _resident((1, D_MODEL)),
            _resident((1, D_MODEL)),
        ],
        out_specs=out_tile,
        out_shape=jax.ShapeDtypeStruct(x.shape, F32),
        scratch_shapes=[
            pltpu.VMEM((nblk, N_KV_HEADS, 2 * BLOCK, LANES), BF16),
            pltpu.VMEM((tm + BLOCK, 2 * N_KV_HEADS * LANES), BF16),
            pltpu.VMEM((tm + BLOCK, N_KV_HEADS * LANES), BF16),
            pltpu.VMEM((SUBLANES, D_MODEL), F32),
            pltpu.VMEM((tm, D_MODEL), BF16),
            pltpu.VMEM((tm, D_MODEL), F32),
            pltpu.VMEM((tm, D_FF), BF16),
        ],
        compiler_params=pltpu.CompilerParams(
            dimension_semantics=("arbitrary",), vmem_limit_bytes=VMEM_LIMIT),
        name="block",
    )(sinks, x, w_in, conv_w, w_out, g1, b1, jnp.asarray(_alibi_bias()), w_gate, w_up, w_down, g2, b2)


def kernel(x, w_in, conv_w, attn_sinks, w_out, ln1_g, ln1_b, w_gate, w_up, w_down, ln2_g, ln2_b):
    bsz, seq, _ = x.shape
    assert seq % TILE == 0 and TILE % BLOCK == 0
    x = x.reshape(bsz * seq, D_MODEL)
    for l in range(DEPTH):
        x = _block_call(x, attn_sinks[l], w_in[l].astype(BF16), conv_w[l], w_out[l].astype(BF16),
                        ln1_g[l][None, :], ln1_b[l][None, :],
                        w_gate[l].astype(BF16), w_up[l].astype(BF16), w_down[l].astype(BF16),
                        ln2_g[l][None, :], ln2_b[l][None, :], seq)
    return x.reshape(bsz, seq, D_MODEL)
```

```python
import functools

import numpy as np

import jax
import jax.numpy as jnp
from jax import lax
from jax.experimental import pallas as pl
from jax.experimental.pallas import tpu as pltpu

D_MODEL = 1024
HEAD_DIM = 64
N_Q_HEADS = D_MODEL // HEAD_DIM
N_KV_HEADS = N_Q_HEADS // 4
GQA_GROUP = N_Q_HEADS // N_KV_HEADS
KV_WIDTH = N_KV_HEADS * HEAD_DIM
WINDOW = 128
BLOCK = 128
CONV_K = 3
D_FF = 2816
DEPTH = 1
ALPHA = (2.0 * DEPTH) ** 0.25
LN_EPS = 1e-5
NEG_INF = -1e30

OFF_Q = 0
OFF_K = OFF_Q + D_MODEL
OFF_V = OFF_K + KV_WIDTH
OFF_CB = OFF_V + KV_WIDTH
OFF_CC = OFF_CB + D_MODEL
OFF_CH = OFF_CC + D_MODEL
OFF_GC = OFF_CH + D_MODEL
OFF_GA = OFF_GC + D_MODEL
IN_WIDTH = OFF_GA + D_MODEL

LANES = 128
SUBLANES = 8
TILE = 256
CONV_CHUNK = GQA_GROUP * HEAD_DIM
FFN_CHUNK = 256
N_FFN_CHUNKS = D_FF // FFN_CHUNK
FFN_HEAD_CHUNKS = N_FFN_CHUNKS - 2 * N_KV_HEADS
VMEM_LIMIT = 56 * 1024 * 1024

F32 = jnp.float32
BF16 = jnp.bfloat16


def _alibi_bias():
    slopes = np.array([2.0 ** (-8.0 * (h + 1) / N_Q_HEADS) for h in range(N_Q_HEADS)], dtype=np.float32)
    kj = np.arange(BLOCK)[:, None]
    qi = np.arange(BLOCK)[None, :]
    dist = ((qi - kj) % BLOCK).astype(np.float32)
    return -slopes[:, None, None] * dist[None]


def _dot(a, b):
    return jnp.dot(a, b, preferred_element_type=F32)


def _dot_nt(a, b):
    return lax.dot_general(a, b, (((1,), (1,)), ((), ())), preferred_element_type=F32)


def _layer_norm(z, g, b):
    mu = jnp.mean(z, axis=-1, keepdims=True)
    d = z - mu
    var = jnp.mean(d * d, axis=-1, keepdims=True)
    return d * lax.rsqrt(var + LN_EPS) * g + b


def _block_kernel(sinks_ref, x_ref, w_in_ref, conv_w_ref, w_out_ref, g1_ref, b1_ref, bias_ref,
                  w_gate_ref, w_up_ref, w_down_ref, g2_ref, b2_ref, o_ref,
                  q_scr, k_scr, vt_scr, ucarry_scr, merged_scr, z1_scr, hid_scr, *, tiles_per_seq):
    tm = x_ref.shape[0]
    nblk = tm // BLOCK
    s = pl.program_id(0)
    i = lax.rem(s, tiles_per_seq)
    seq_start = i == 0

    @pl.when(s == 0)
    def _():
        z1_scr[...] = jnp.zeros(z1_scr.shape, F32)
        k_scr[...] = jnp.zeros(k_scr.shape, BF16)
        vt_scr[...] = jnp.zeros(vt_scr.shape, BF16)
        ucarry_scr[...] = jnp.zeros(ucarry_scr.shape, F32)

    x1 = _layer_norm(z1_scr[...], g1_ref[...], b1_ref[...])
    x1b = x1.astype(BF16)

    def gate_up(c):
        c0 = c * FFN_CHUNK
        gate = _dot(x1b, w_gate_ref[:, c0:c0 + FFN_CHUNK])
        up = _dot(x1b, w_up_ref[:, c0:c0 + FFN_CHUNK])
        hid_scr[:, c0:c0 + FFN_CHUNK] = (gate * jax.nn.sigmoid(gate) * up).astype(BF16)

    k_scr[0:BLOCK, :] = jnp.where(seq_start, jnp.zeros((BLOCK, k_scr.shape[1]), BF16), k_scr[tm:tm + BLOCK, :])
    vt_scr[:, 0:BLOCK] = jnp.where(seq_start, jnp.zeros((vt_scr.shape[0], BLOCK), BF16), vt_scr[:, tm:tm + BLOCK])

    xb = x_ref[...].astype(BF16)
    lane = lax.broadcasted_iota(jnp.int32, (tm, LANES), 1)
    lo = lane < HEAD_DIM

    qkv = _dot(xb, w_in_ref[:, OFF_Q:OFF_CB])
    for blk in range(nblk):
        for pr in range(N_Q_HEADS // 2):
            qp = qkv[blk * BLOCK:(blk + 1) * BLOCK, pr * LANES:(pr + 1) * LANES] * (HEAD_DIM ** -0.5)
            q_scr[blk, pr // 2, (pr % 2) * BLOCK:(pr % 2 + 1) * BLOCK, :] = qp.astype(BF16)

    for t in range(KV_WIDTH // LANES):
        kt = qkv[:, OFF_K + t * LANES:OFF_K + (t + 1) * LANES]
        kr = pltpu.roll(kt, HEAD_DIM, axis=1)
        zero = jnp.zeros_like(kt)
        h0, h1 = 2 * t, 2 * t + 1
        k_scr[BLOCK:, (2 * h0) * LANES:(2 * h0 + 1) * LANES] = jnp.where(lo, kt, zero).astype(BF16)
        k_scr[BLOCK:, (2 * h0 + 1) * LANES:(2 * h0 + 2) * LANES] = jnp.where(lo, zero, kr).astype(BF16)
        k_scr[BLOCK:, (2 * h1) * LANES:(2 * h1 + 1) * LANES] = jnp.where(lo, kr, zero).astype(BF16)
        k_scr[BLOCK:, (2 * h1 + 1) * LANES:(2 * h1 + 2) * LANES] = jnp.where(lo, zero, kt).astype(BF16)
    v_t = qkv[:, OFF_V:OFF_V + KV_WIDTH].T
    for h in range(N_KV_HEADS):
        vh = v_t[h * HEAD_DIM:(h + 1) * HEAD_DIM, :].astype(BF16)
        vt_scr[(2 * h) * HEAD_DIM:(2 * h + 1) * HEAD_DIM, BLOCK:] = vh
        vt_scr[(2 * h + 1) * HEAD_DIM:(2 * h + 2) * HEAD_DIM, BLOCK:] = vh

    for c in range(FFN_HEAD_CHUNKS):
        gate_up(c)

    kj = lax.broadcasted_iota(jnp.int32, (BLOCK, BLOCK), 0)
    qi = lax.broadcasted_iota(jnp.int32, (BLOCK, BLOCK), 1)
    tri = kj <= qi
    top = kj < HEAD_DIM
    valid0 = (kj - qi) <= jnp.where(seq_start, 0, BLOCK)
    row = lax.broadcasted_iota(jnp.int32, (tm, CONV_CHUNK), 0)

    def scores(blk, h):
        r0 = blk * BLOCK
        qg = q_scr[blk, h]
        ke = k_scr[r0:r0 + 2 * BLOCK, (2 * h) * LANES:(2 * h + 1) * LANES]
        ko = k_scr[r0:r0 + 2 * BLOCK, (2 * h + 1) * LANES:(2 * h + 2) * LANES]
        se = _dot_nt(ke, qg)
        so = _dot_nt(ko, qg)
        return se, so

    def attend(blk, h, se, so):
        r0 = blk * BLOCK
        probs, inv = [], []
        for g, (sc, half) in enumerate(((se, 0), (so, 0), (se, 1), (so, 1))):
            head = GQA_GROUP * h + g
            s_prev = sc[0:BLOCK, half * BLOCK:(half + 1) * BLOCK]
            s_cur = sc[BLOCK:2 * BLOCK, half * BLOCK:(half + 1) * BLOCK]
            sm = jnp.where(tri, s_cur, s_prev) + bias_ref[head]
            if blk == 0:
                sm = jnp.where(valid0, sm, NEG_INF)
            sink = sinks_ref[head]
            m = jnp.maximum(jnp.max(sm, axis=0, keepdims=True), sink)
            p = jnp.exp(sm - m)
            denom = jnp.sum(p, axis=0, keepdims=True) + jnp.exp(sink - m)
            inv.append(1.0 / denom)
            zero = jnp.zeros_like(p)
            probs.append(jnp.concatenate([jnp.where(tri, zero, p), jnp.where(tri, p, zero)],
                                         axis=0).astype(BF16))
        p_t = jnp.concatenate(probs, axis=1)
        vt = vt_scr[(2 * h) * HEAD_DIM:(2 * h + 2) * HEAD_DIM, r0:r0 + 2 * BLOCK]
        o_t = _dot(vt, p_t) * jnp.concatenate(inv, axis=1)
        pairs = []
        for pr in range(2):
            oe = o_t[:, (2 * pr) * BLOCK:(2 * pr + 1) * BLOCK]
            oo = o_t[:, (2 * pr + 1) * BLOCK:(2 * pr + 2) * BLOCK]
            pairs.append(jnp.where(top, oe, oo).T)
        return jnp.concatenate(pairs, axis=1)

    def proj(off, h):
        c0 = off + h * CONV_CHUNK
        return _dot(xb, w_in_ref[:, c0:c0 + CONV_CHUNK])

    units = [(h, blk) for h in range(N_KV_HEADS) for blk in range(nblk)]
    assert nblk == 2
    sc = scores(0, 0)
    for n, (h, blk) in enumerate(units):
        c0 = h * CONV_CHUNK
        if blk == 0:
            cb, cc, ch = proj(OFF_CB, h), proj(OFF_CC, h), proj(OFF_CH, h)
            y_blocks = []
        else:
            gc, ga = proj(OFF_GC, h), proj(OFF_GA, h)
        gate_up(FFN_HEAD_CHUNKS + n)
        if n + 1 < len(units):
            sc_next = scores(units[n + 1][1], units[n + 1][0])
        y_blocks.append(attend(blk, h, *sc))
        sc = sc_next
        if blk == 0:
            continue
        y_attn = jnp.concatenate(y_blocks, axis=0)
        u = cc * ch
        zrow = jnp.zeros((1, CONV_CHUNK), F32)
        last = jnp.where(seq_start, zrow, ucarry_scr[SUBLANES - 1:SUBLANES, c0:c0 + CONV_CHUNK])
        prev = jnp.where(seq_start, zrow, ucarry_scr[SUBLANES - 2:SUBLANES - 1, c0:c0 + CONV_CHUNK])
        u1 = jnp.where(row == 0, last, pltpu.roll(u, 1, axis=0))
        u2 = jnp.where(row == 0, prev, jnp.where(row == 1, last, pltpu.roll(u, 2, axis=0)))
        w = conv_w_ref[:, c0:c0 + CONV_CHUNK]
        y = w[0:1, :] * u2 + w[1:2, :] * u1 + w[2:3, :] * u
        ucarry_scr[:, c0:c0 + CONV_CHUNK] = u[tm - SUBLANES:tm, :]
        merged = jax.nn.sigmoid(gc) * (cb * y) + jax.nn.sigmoid(ga) * y_attn
        merged_scr[:, c0:c0 + CONV_CHUNK] = merged.astype(BF16)

    ffn = _dot(hid_scr[...], w_down_ref[...])
    o_ref[...] = _layer_norm(ALPHA * x1 + ffn, g2_ref[...], b2_ref[...])

    mix = _dot(merged_scr[...], w_out_ref[...])
    z1_scr[...] = ALPHA * x_ref[...] + mix


def _resident(shape):
    return pl.BlockSpec(shape, lambda *_: (0,) * len(shape), pipeline_mode=pl.Buffered(1))


def _block_call(x, sinks, w_in, conv_w, w_out, g1, b1, w_gate, w_up, w_down, g2, b2, seq):
    tokens = x.shape[0]
    tm = TILE
    nblk = tm // BLOCK
    n_tiles = tokens // tm
    in_tile = pl.BlockSpec((tm, D_MODEL), lambda s: (jnp.minimum(s, n_tiles - 1), 0))
    out_tile = pl.BlockSpec((tm, D_MODEL), lambda s: (jnp.maximum(s - 1, 0), 0))
    return pl.pallas_call(
        functools.partial(_block_kernel, tiles_per_seq=seq // tm),
        grid=(n_tiles + 1,),
        in_specs=[
            pl.BlockSpec(memory_space=pltpu.SMEM),
            in_tile,
            _resident((D_MODEL, IN_WIDTH)),
            _resident((CONV_K, D_MODEL)),
            _resident((D_MODEL, D_MODEL)),
            _resident((1, D_MODEL)),
            _resident((1, D_MODEL)),
            _resident((N_Q_HEADS, BLOCK, BLOCK)),
            _resident((D_MODEL, D_FF)),
            _resident((D_MODEL, D_FF)),
            _resident((D_FF, D_MODEL)),
            _resident((1, D_MODEL)),
            _resident((1, D_MODEL)),
        ],
        out_specs=out_tile,
        out_shape=jax.ShapeDtypeStruct(x.shape, F32),
        scratch_shapes=[
            pltpu.VMEM((nblk, N_KV_HEADS, 2 * BLOCK, LANES), BF16),
            pltpu.VMEM((tm + BLOCK, 2 * N_KV_HEADS * LANES), BF16),
            pltpu.VMEM((N_KV_HEADS * 2 * HEAD_DIM, tm + BLOCK), BF16),
            pltpu.VMEM((SUBLANES, D_MODEL), F32),
            pltpu.VMEM((tm, D_MODEL), BF16),
            pltpu.VMEM((tm, D_MODEL), F32),
            pltpu.VMEM((tm, D_FF), BF16),
        ],
        compiler_params=pltpu.CompilerParams(
            dimension_semantics=("arbitrary",), vmem_limit_bytes=VMEM_LIMIT),
        name="block",
    )(sinks, x, w_in, conv_w, w_out, g1, b1, jnp.asarray(_alibi_bias()), w_gate, w_up, w_down, g2, b2)


def kernel(x, w_in, conv_w, attn_sinks, w_out, ln1_g, ln1_b, w_gate, w_up, w_down, ln2_g, ln2_b):
    bsz, seq, _ = x.shape
    assert seq % TILE == 0 and TILE % BLOCK == 0
    x = x.reshape(bsz * seq, D_MODEL)
    for l in range(DEPTH):
        x = _block_call(x, attn_sinks[l], w_in[l].astype(BF16), conv_w[l], w_out[l].astype(BF16),
                        ln1_g[l][None, :], ln1_b[l][None, :],
                        w_gate[l].astype(BF16), w_up[l].astype(BF16), w_down[l].astype(BF16),
                        ln2_g[l][None, :], ln2_b[l][None, :], seq)
    return x.reshape(bsz, seq, D_MODEL)
```

```python
import functools

import numpy as np

import jax
import jax.numpy as jnp
from jax import lax
from jax.experimental import pallas as pl
from jax.experimental.pallas import tpu as pltpu

D_MODEL = 1024
HEAD_DIM = 64
N_Q_HEADS = D_MODEL // HEAD_DIM
N_KV_HEADS = N_Q_HEADS // 4
GQA_GROUP = N_Q_HEADS // N_KV_HEADS
KV_WIDTH = N_KV_HEADS * HEAD_DIM
WINDOW = 128
BLOCK = 128
CONV_K = 3
D_FF = 2816
DEPTH = 1
ALPHA = (2.0 * DEPTH) ** 0.25
LN_EPS = 1e-5
NEG_INF = -1e30

OFF_Q = 0
OFF_K = OFF_Q + D_MODEL
OFF_V = OFF_K + KV_WIDTH
OFF_CB = OFF_V + KV_WIDTH
OFF_CC = OFF_CB + D_MODEL
OFF_CH = OFF_CC + D_MODEL
OFF_GC = OFF_CH + D_MODEL
OFF_GA = OFF_GC + D_MODEL
IN_WIDTH = OFF_GA + D_MODEL

LANES = 128
SUBLANES = 8
TILE = 256
CONV_CHUNK = GQA_GROUP * HEAD_DIM
FFN_CHUNK = 256
N_FFN_CHUNKS = D_FF // FFN_CHUNK
FFN_HEAD_CHUNKS = N_FFN_CHUNKS - 2 * N_KV_HEADS
VMEM_LIMIT = 56 * 1024 * 1024

F32 = jnp.float32
BF16 = jnp.bfloat16


def _alibi_bias():
    slopes = np.array([2.0 ** (-8.0 * (h + 1) / N_Q_HEADS) for h in range(N_Q_HEADS)], dtype=np.float32)
    kj = np.arange(BLOCK)[:, None]
    qi = np.arange(BLOCK)[None, :]
    dist = ((qi - kj) % BLOCK).astype(np.float32)
    return -slopes[:, None, None] * dist[None]


def _dot(a, b):
    return jnp.dot(a, b, preferred_element_type=F32)


def _dot_nt(a, b):
    return lax.dot_general(a, b, (((1,), (1,)), ((), ())), preferred_element_type=F32)


def _layer_norm(z, g, b):
    mu = jnp.mean(z, axis=-1, keepdims=True)
    d = z - mu
    var = jnp.mean(d * d, axis=-1, keepdims=True)
    return d * lax.rsqrt(var + LN_EPS) * g + b


def _block_kernel(sinks_ref, x_ref, w_in_ref, conv_w_ref, w_out_ref, g1_ref, b1_ref, bias_ref,
                  w_gate_ref, w_up_ref, w_down_ref, g2_ref, b2_ref, o_ref,
                  q_scr, k_scr, vt_scr, ucarry_scr, merged_scr, z1_scr, hid_scr, *, tiles_per_seq):
    tm = x_ref.shape[0]
    nblk = tm // BLOCK
    assert nblk == 2
    units = [(h, blk) for h in range(N_KV_HEADS) for blk in range(nblk)]
    s = pl.program_id(0)
    last = pl.num_programs(0) - 1
    seq_start = lax.rem(s, tiles_per_seq) == 0

    def tile_step(mix, ffn):
        if ffn:
            x1 = _layer_norm(z1_scr[...], g1_ref[...], b1_ref[...])
            x1b = x1.astype(BF16)

        def gate_up(c):
            if not ffn:
                return
            c0 = c * FFN_CHUNK
            gate = _dot(x1b, w_gate_ref[:, c0:c0 + FFN_CHUNK])
            up = _dot(x1b, w_up_ref[:, c0:c0 + FFN_CHUNK])
            hid_scr[:, c0:c0 + FFN_CHUNK] = (gate * jax.nn.sigmoid(gate) * up).astype(BF16)

        if mix:
            xb = x_ref[...].astype(BF16)
            mix_projections(xb)
        for c in range(FFN_HEAD_CHUNKS):
            gate_up(c)
        if mix:
            mix_heads(xb, lambda n: gate_up(FFN_HEAD_CHUNKS + n))
        else:
            for n in range(len(units)):
                gate_up(FFN_HEAD_CHUNKS + n)
        if ffn:
            ffn_out = _dot(hid_scr[...], w_down_ref[...])
            o_ref[...] = _layer_norm(ALPHA * x1 + ffn_out, g2_ref[...], b2_ref[...])
        if mix:
            mix_out = _dot(merged_scr[...], w_out_ref[...])
            z1_scr[...] = ALPHA * x_ref[...] + mix_out

    def mix_projections(xb):
        k_scr[0:BLOCK, :] = jnp.where(seq_start, jnp.zeros((BLOCK, k_scr.shape[1]), BF16), k_scr[tm:tm + BLOCK, :])
        vt_scr[:, 0:BLOCK] = jnp.where(seq_start, jnp.zeros((vt_scr.shape[0], BLOCK), BF16),
                                       vt_scr[:, tm:tm + BLOCK])
        lo = lax.broadcasted_iota(jnp.int32, (tm, LANES), 1) < HEAD_DIM
        qkv = _dot(xb, w_in_ref[:, OFF_Q:OFF_CB])
        for blk in range(nblk):
            for pr in range(N_Q_HEADS // 2):
                qp = qkv[blk * BLOCK:(blk + 1) * BLOCK, pr * LANES:(pr + 1) * LANES] * (HEAD_DIM ** -0.5)
                q_scr[blk, pr // 2, (pr % 2) * BLOCK:(pr % 2 + 1) * BLOCK, :] = qp.astype(BF16)
        for t in range(KV_WIDTH // LANES):
            kt = qkv[:, OFF_K + t * LANES:OFF_K + (t + 1) * LANES]
            kr = pltpu.roll(kt, HEAD_DIM, axis=1)
            zero = jnp.zeros_like(kt)
            h0, h1 = 2 * t, 2 * t + 1
            k_scr[BLOCK:, (2 * h0) * LANES:(2 * h0 + 1) * LANES] = jnp.where(lo, kt, zero).astype(BF16)
            k_scr[BLOCK:, (2 * h0 + 1) * LANES:(2 * h0 + 2) * LANES] = jnp.where(lo, zero, kr).astype(BF16)
            k_scr[BLOCK:, (2 * h1) * LANES:(2 * h1 + 1) * LANES] = jnp.where(lo, kr, zero).astype(BF16)
            k_scr[BLOCK:, (2 * h1 + 1) * LANES:(2 * h1 + 2) * LANES] = jnp.where(lo, zero, kt).astype(BF16)
        v_t = qkv[:, OFF_V:OFF_V + KV_WIDTH].T
        for h in range(N_KV_HEADS):
            vh = v_t[h * HEAD_DIM:(h + 1) * HEAD_DIM, :].astype(BF16)
            vt_scr[(2 * h) * HEAD_DIM:(2 * h + 1) * HEAD_DIM, BLOCK:] = vh
            vt_scr[(2 * h + 1) * HEAD_DIM:(2 * h + 2) * HEAD_DIM, BLOCK:] = vh

    def mix_heads(xb, filler):
        kj = lax.broadcasted_iota(jnp.int32, (BLOCK, BLOCK), 0)
        qi = lax.broadcasted_iota(jnp.int32, (BLOCK, BLOCK), 1)
        tri = kj <= qi
        top = kj < HEAD_DIM
        valid0 = (kj - qi) <= jnp.where(seq_start, 0, BLOCK)
        row = lax.broadcasted_iota(jnp.int32, (tm, CONV_CHUNK), 0)

        def scores(h, blk):
            r0 = blk * BLOCK
            qg = q_scr[blk, h]
            ke = k_scr[r0:r0 + 2 * BLOCK, (2 * h) * LANES:(2 * h + 1) * LANES]
            ko = k_scr[r0:r0 + 2 * BLOCK, (2 * h + 1) * LANES:(2 * h + 2) * LANES]
            se = _dot_nt(ke, qg)
            so = _dot_nt(ko, qg)
            return se, so

        def attend(h, blk, se, so):
            r0 = blk * BLOCK
            probs, inv = [], []
            for g, (sc, half) in enumerate(((se, 0), (so, 0), (se, 1), (so, 1))):
                head = GQA_GROUP * h + g
                s_prev = sc[0:BLOCK, half * BLOCK:(half + 1) * BLOCK]
                s_cur = sc[BLOCK:2 * BLOCK, half * BLOCK:(half + 1) * BLOCK]
                sm = jnp.where(tri, s_cur, s_prev) + bias_ref[head]
                if blk == 0:
                    sm = jnp.where(valid0, sm, NEG_INF)
                sink = sinks_ref[head]
                m = jnp.maximum(jnp.max(sm, axis=0, keepdims=True), sink)
                p = jnp.exp(sm - m)
                denom = jnp.sum(p, axis=0, keepdims=True) + jnp.exp(sink - m)
                inv.append(1.0 / denom)
                zero = jnp.zeros_like(p)
                probs.append(jnp.concatenate([jnp.where(tri, zero, p), jnp.where(tri, p, zero)],
                                             axis=0).astype(BF16))
            p_t = jnp.concatenate(probs, axis=1)
            vt = vt_scr[(2 * h) * HEAD_DIM:(2 * h + 2) * HEAD_DIM, r0:r0 + 2 * BLOCK]
            o_t = _dot(vt, p_t) * jnp.concatenate(inv, axis=1)
            pairs = []
            for pr in range(2):
                oe = o_t[:, (2 * pr) * BLOCK:(2 * pr + 1) * BLOCK]
                oo = o_t[:, (2 * pr + 1) * BLOCK:(2 * pr + 2) * BLOCK]
                pairs.append(jnp.where(top, oe, oo).T)
            return jnp.concatenate(pairs, axis=1)

        def proj(off, h):
            c0 = off + h * CONV_CHUNK
            return _dot(xb, w_in_ref[:, c0:c0 + CONV_CHUNK])

        sc = scores(*units[0])
        for n, (h, blk) in enumerate(units):
            c0 = h * CONV_CHUNK
            if blk == 0:
                cb, cc, ch = proj(OFF_CB, h), proj(OFF_CC, h), proj(OFF_CH, h)
                y_blocks = []
            else:
                gc, ga = proj(OFF_GC, h), proj(OFF_GA, h)
            filler(n)
            if n + 1 < len(units):
                sc_next = scores(*units[n + 1])
            y_blocks.append(attend(h, blk, *sc))
            sc = sc_next
            if blk == 0:
                continue
            y_attn = jnp.concatenate(y_blocks, axis=0)
            u = cc * ch
            zrow = jnp.zeros((1, CONV_CHUNK), F32)
            last_u = jnp.where(seq_start, zrow, ucarry_scr[SUBLANES - 1:SUBLANES, c0:c0 + CONV_CHUNK])
            prev_u = jnp.where(seq_start, zrow, ucarry_scr[SUBLANES - 2:SUBLANES - 1, c0:c0 + CONV_CHUNK])
            u1 = jnp.where(row == 0, last_u, pltpu.roll(u, 1, axis=0))
            u2 = jnp.where(row == 0, prev_u, jnp.where(row == 1, last_u, pltpu.roll(u, 2, axis=0)))
            w = conv_w_ref[:, c0:c0 + CONV_CHUNK]
            y = w[0:1, :] * u2 + w[1:2, :] * u1 + w[2:3, :] * u
            ucarry_scr[:, c0:c0 + CONV_CHUNK] = u[tm - SUBLANES:tm, :]
            merged = jax.nn.sigmoid(gc) * (cb * y) + jax.nn.sigmoid(ga) * y_attn
            merged_scr[:, c0:c0 + CONV_CHUNK] = merged.astype(BF16)

    @pl.when(s == 0)
    def _():
        k_scr[...] = jnp.zeros(k_scr.shape, BF16)
        vt_scr[...] = jnp.zeros(vt_scr.shape, BF16)
        ucarry_scr[...] = jnp.zeros(ucarry_scr.shape, F32)
        tile_step(mix=True, ffn=False)

    @pl.when(jnp.logical_and(s > 0, s < last))
    def _():
        tile_step(mix=True, ffn=True)

    @pl.when(s == last)
    def _():
        tile_step(mix=False, ffn=True)


def _resident(shape):
    return pl.BlockSpec(shape, lambda *_: (0,) * len(shape), pipeline_mode=pl.Buffered(1))


def _block_call(x, sinks, w_in, conv_w, w_out, g1, b1, w_gate, w_up, w_down, g2, b2, seq):
    tokens = x.shape[0]
    tm = TILE
    nblk = tm // BLOCK
    n_tiles = tokens // tm
    in_tile = pl.BlockSpec((tm, D_MODEL), lambda s: (jnp.minimum(s, n_tiles - 1), 0))
    out_tile = pl.BlockSpec((tm, D_MODEL), lambda s: (jnp.maximum(s - 1, 0), 0))
    return pl.pallas_call(
        functools.partial(_block_kernel, tiles_per_seq=seq // tm),
        grid=(n_tiles + 1,),
        in_specs=[
            pl.BlockSpec(memory_space=pltpu.SMEM),
            in_tile,
            _resident((D_MODEL, IN_WIDTH)),
            _resident((CONV_K, D_MODEL)),
            _resident((D_MODEL, D_MODEL)),
            _resident((1, D_MODEL)),
            _resident((1, D_MODEL)),
            _resident((N_Q_HEADS, BLOCK, BLOCK)),
            _resident((D_MODEL, D_FF)),
            _resident((D_MODEL, D_FF)),
            _resident((D_FF, D_MODEL)),
            _resident((1, D_MODEL)),
            _resident((1, D_MODEL)),
        ],
        out_specs=out_tile,
        out_shape=jax.ShapeDtypeStruct(x.shape, F32),
        scratch_shapes=[
            pltpu.VMEM((nblk, N_KV_HEADS, 2 * BLOCK, LANES), BF16),
            pltpu.VMEM((tm + BLOCK, 2 * N_KV_HEADS * LANES), BF16),
            pltpu.VMEM((N_KV_HEADS * 2 * HEAD_DIM, tm + BLOCK), BF16),
            pltpu.VMEM((SUBLANES, D_MODEL), F32),
            pltpu.VMEM((tm, D_MODEL), BF16),
            pltpu.VMEM((tm, D_MODEL), F32),
            pltpu.VMEM((tm, D_FF), BF16),
        ],
        compiler_params=pltpu.CompilerParams(
            dimension_semantics=("arbitrary",), vmem_limit_bytes=VMEM_LIMIT),
        name="block",
    )(sinks, x, w_in, conv_w, w_out, g1, b1, jnp.asarray(_alibi_bias()), w_gate, w_up, w_down, g2, b2)


def kernel(x, w_in, conv_w, attn_sinks, w_out, ln1_g, ln1_b, w_gate, w_up, w_down, ln2_g, ln2_b):
    bsz, seq, _ = x.shape
    assert seq % TILE == 0 and TILE % BLOCK == 0
    x = x.reshape(bsz * seq, D_MODEL)
    for l in range(DEPTH):
        x = _block_call(x, attn_sinks[l], w_in[l].astype(BF16), conv_w[l], w_out[l].astype(BF16),
                        ln1_g[l][None, :], ln1_b[l][None, :],
                        w_gate[l].astype(BF16), w_up[l].astype(BF16), w_down[l].astype(BF16),
                        ln2_g[l][None, :], ln2_b[l][None, :], seq)
    return x.reshape(bsz, seq, D_MODEL)
```

```python
import functools

import numpy as np

import jax
import jax.numpy as jnp
from jax import lax
from jax.experimental import pallas as pl
from jax.experimental.pallas import tpu as pltpu

D_MODEL = 1024
HEAD_DIM = 64
N_Q_HEADS = D_MODEL // HEAD_DIM
N_KV_HEADS = N_Q_HEADS // 4
GQA_GROUP = N_Q_HEADS // N_KV_HEADS
KV_WIDTH = N_KV_HEADS * HEAD_DIM
WINDOW = 128
BLOCK = 128
CONV_K = 3
D_FF = 2816
DEPTH = 1
ALPHA = (2.0 * DEPTH) ** 0.25
LN_EPS = 1e-5
NEG_INF = -1e30

OFF_Q = 0
OFF_K = OFF_Q + D_MODEL
OFF_V = OFF_K + KV_WIDTH
OFF_CB = OFF_V + KV_WIDTH
OFF_CC = OFF_CB + D_MODEL
OFF_CH = OFF_CC + D_MODEL
OFF_GC = OFF_CH + D_MODEL
OFF_GA = OFF_GC + D_MODEL
IN_WIDTH = OFF_GA + D_MODEL

LANES = 128
SUBLANES = 8
TILE = 256
CONV_CHUNK = GQA_GROUP * HEAD_DIM
FFN_CHUNK = 256
N_FFN_CHUNKS = D_FF // FFN_CHUNK
FFN_HEAD_CHUNKS = N_FFN_CHUNKS - 2 * N_KV_HEADS
VMEM_LIMIT = 58 * 1024 * 1024
CAST_ROWS_IN = 32
CAST_ROWS_FF = 64
CAST_ROWS_SQ = 128

F32 = jnp.float32
BF16 = jnp.bfloat16


def _alibi_bias():
    slopes = np.array([2.0 ** (-8.0 * (h + 1) / N_Q_HEADS) for h in range(N_Q_HEADS)], dtype=np.float32)
    kj = np.arange(BLOCK)[:, None]
    qi = np.arange(BLOCK)[None, :]
    dist = ((qi - kj) % BLOCK).astype(np.float32)
    return -slopes[:, None, None] * dist[None]


def _dot(a, b):
    return jnp.dot(a, b, preferred_element_type=F32)


def _dot_nt(a, b):
    return lax.dot_general(a, b, (((1,), (1,)), ((), ())), preferred_element_type=F32)


def _layer_norm(z, g, b):
    mu = jnp.mean(z, axis=-1, keepdims=True)
    d = z - mu
    var = jnp.mean(d * d, axis=-1, keepdims=True)
    return d * lax.rsqrt(var + LN_EPS) * g + b


def _cast_weight(src_hbm, dst_scr, stage, sem):
    rows = stage.shape[1]
    n = src_hbm.shape[0] // rows
    assert n * rows == src_hbm.shape[0] and stage.shape[2] == src_hbm.shape[1]

    def chunk_copy(k, slot):
        return pltpu.make_async_copy(src_hbm.at[pl.ds(k * rows, rows), :], stage.at[slot], sem.at[slot])

    chunk_copy(0, 0).start()

    def body(k, carry):
        slot = lax.rem(k, 2)

        @pl.when(k + 1 < n)
        def _():
            chunk_copy(k + 1, 1 - slot).start()

        chunk_copy(k, slot).wait()
        dst_scr[pl.ds(pl.multiple_of(k * rows, rows), rows), :] = stage[slot].astype(BF16)
        return carry

    lax.fori_loop(0, n, body, 0)


def _block_kernel(sinks_ref, x_ref, w_in_hbm, conv_w_ref, w_out_hbm, g1_ref, b1_ref, bias_ref,
                  w_gate_hbm, w_up_hbm, w_down_hbm, g2_ref, b2_ref, o_ref,
                  q_scr, k_scr, vt_scr, ucarry_scr, merged_scr, z1_scr, hid_scr,
                  w_in_ref, w_out_ref, w_gate_ref, w_up_ref, w_down_ref, stage_in, stage_ff, stage_sq, cast_sem,
                  *, tiles_per_seq):
    tm = x_ref.shape[0]
    nblk = tm // BLOCK
    s = pl.program_id(0)
    i = lax.rem(s, tiles_per_seq)
    seq_start = i == 0

    @pl.when(s == 0)
    def _():
        z1_scr[...] = jnp.zeros(z1_scr.shape, F32)
        k_scr[...] = jnp.zeros(k_scr.shape, BF16)
        vt_scr[...] = jnp.zeros(vt_scr.shape, BF16)
        ucarry_scr[...] = jnp.zeros(ucarry_scr.shape, F32)
        _cast_weight(w_in_hbm, w_in_ref, stage_in, cast_sem)
        _cast_weight(w_out_hbm, w_out_ref, stage_sq, cast_sem)
        _cast_weight(w_gate_hbm, w_gate_ref, stage_ff, cast_sem)
        _cast_weight(w_up_hbm, w_up_ref, stage_ff, cast_sem)
        _cast_weight(w_down_hbm, w_down_ref, stage_sq, cast_sem)

    x1 = _layer_norm(z1_scr[...], g1_ref[...], b1_ref[...])
    x1b = x1.astype(BF16)

    def gate_up(c):
        c0 = c * FFN_CHUNK
        gate = _dot(x1b, w_gate_ref[:, c0:c0 + FFN_CHUNK])
        up = _dot(x1b, w_up_ref[:, c0:c0 + FFN_CHUNK])
        hid_scr[:, c0:c0 + FFN_CHUNK] = (gate * jax.nn.sigmoid(gate) * up).astype(BF16)

    k_scr[0:BLOCK, :] = jnp.where(seq_start, jnp.zeros((BLOCK, k_scr.shape[1]), BF16), k_scr[tm:tm + BLOCK, :])
    vt_scr[:, 0:BLOCK] = jnp.where(seq_start, jnp.zeros((vt_scr.shape[0], BLOCK), BF16), vt_scr[:, tm:tm + BLOCK])

    xb = x_ref[...].astype(BF16)
    lane = lax.broadcasted_iota(jnp.int32, (tm, LANES), 1)
    lo = lane < HEAD_DIM

    qkv = _dot(xb, w_in_ref[:, OFF_Q:OFF_CB])
    for blk in range(nblk):
        for pr in range(N_Q_HEADS // 2):
            qp = qkv[blk * BLOCK:(blk + 1) * BLOCK, pr * LANES:(pr + 1) * LANES] * (HEAD_DIM ** -0.5)
            q_scr[blk, pr // 2, (pr % 2) * BLOCK:(pr % 2 + 1) * BLOCK, :] = qp.astype(BF16)

    for t in range(KV_WIDTH // LANES):
        kt = qkv[:, OFF_K + t * LANES:OFF_K + (t + 1) * LANES]
        kr = pltpu.roll(kt, HEAD_DIM, axis=1)
        zero = jnp.zeros_like(kt)
        h0, h1 = 2 * t, 2 * t + 1
        k_scr[BLOCK:, (2 * h0) * LANES:(2 * h0 + 1) * LANES] = jnp.where(lo, kt, zero).astype(BF16)
        k_scr[BLOCK:, (2 * h0 + 1) * LANES:(2 * h0 + 2) * LANES] = jnp.where(lo, zero, kr).astype(BF16)
        k_scr[BLOCK:, (2 * h1) * LANES:(2 * h1 + 1) * LANES] = jnp.where(lo, kr, zero).astype(BF16)
        k_scr[BLOCK:, (2 * h1 + 1) * LANES:(2 * h1 + 2) * LANES] = jnp.where(lo, zero, kt).astype(BF16)
    v_t = qkv[:, OFF_V:OFF_V + KV_WIDTH].T
    for h in range(N_KV_HEADS):
        vh = v_t[h * HEAD_DIM:(h + 1) * HEAD_DIM, :].astype(BF16)
        vt_scr[(2 * h) * HEAD_DIM:(2 * h + 1) * HEAD_DIM, BLOCK:] = vh
        vt_scr[(2 * h + 1) * HEAD_DIM:(2 * h + 2) * HEAD_DIM, BLOCK:] = vh

    for c in range(FFN_HEAD_CHUNKS):
        gate_up(c)

    kj = lax.broadcasted_iota(jnp.int32, (BLOCK, BLOCK), 0)
    qi = lax.broadcasted_iota(jnp.int32, (BLOCK, BLOCK), 1)
    tri = kj <= qi
    top = kj < HEAD_DIM
    valid0 = (kj - qi) <= jnp.where(seq_start, 0, BLOCK)
    row = lax.broadcasted_iota(jnp.int32, (tm, CONV_CHUNK), 0)

    def scores(blk, h):
        r0 = blk * BLOCK
        qg = q_scr[blk, h]
        ke = k_scr[r0:r0 + 2 * BLOCK, (2 * h) * LANES:(2 * h + 1) * LANES]
        ko = k_scr[r0:r0 + 2 * BLOCK, (2 * h + 1) * LANES:(2 * h + 2) * LANES]
        se = _dot_nt(ke, qg)
        so = _dot_nt(ko, qg)
        return se, so

    def attend(blk, h, se, so):
        r0 = blk * BLOCK
        probs, inv = [], []
        for g, (sc, half) in enumerate(((se, 0), (so, 0), (se, 1), (so, 1))):
            head = GQA_GROUP * h + g
            s_prev = sc[0:BLOCK, half * BLOCK:(half + 1) * BLOCK]
            s_cur = sc[BLOCK:2 * BLOCK, half * BLOCK:(half + 1) * BLOCK]
            sm = jnp.where(tri, s_cur, s_prev) + bias_ref[head]
            if blk == 0:
                sm = jnp.where(valid0, sm, NEG_INF)
            sink = sinks_ref[head]
            m = jnp.maximum(jnp.max(sm, axis=0, keepdims=True), sink)
            p = jnp.exp(sm - m)
            denom = jnp.sum(p, axis=0, keepdims=True) + jnp.exp(sink - m)
            inv.append(1.0 / denom)
            zero = jnp.zeros_like(p)
            probs.append(jnp.concatenate([jnp.where(tri, zero, p), jnp.where(tri, p, zero)],
                                         axis=0).astype(BF16))
        p_t = jnp.concatenate(probs, axis=1)
        vt = vt_scr[(2 * h) * HEAD_DIM:(2 * h + 2) * HEAD_DIM, r0:r0 + 2 * BLOCK]
        o_t = _dot(vt, p_t) * jnp.concatenate(inv, axis=1)
        pairs = []
        for pr in range(2):
            oe = o_t[:, (2 * pr) * BLOCK:(2 * pr + 1) * BLOCK]
            oo = o_t[:, (2 * pr + 1) * BLOCK:(2 * pr + 2) * BLOCK]
            pairs.append(jnp.where(top, oe, oo).T)
        return jnp.concatenate(pairs, axis=1)

    def proj(off, h):
        c0 = off + h * CONV_CHUNK
        return _dot(xb, w_in_ref[:, c0:c0 + CONV_CHUNK])

    units = [(h, blk) for h in range(N_KV_HEADS) for blk in range(nblk)]
    assert nblk == 2
    sc = scores(0, 0)
    for n, (h, blk) in enumerate(units):
        c0 = h * CONV_CHUNK
        if blk == 0:
            cb, cc, ch = proj(OFF_CB, h), proj(OFF_CC, h), proj(OFF_CH, h)
            y_blocks = []
        else:
            gc, ga = proj(OFF_GC, h), proj(OFF_GA, h)
        gate_up(FFN_HEAD_CHUNKS + n)
        if n + 1 < len(units):
            sc_next = scores(units[n + 1][1], units[n + 1][0])
        y_blocks.append(attend(blk, h, *sc))
        sc = sc_next
        if blk == 0:
            continue
        y_attn = jnp.concatenate(y_blocks, axis=0)
        u = cc * ch
        zrow = jnp.zeros((1, CONV_CHUNK), F32)
        last = jnp.where(seq_start, zrow, ucarry_scr[SUBLANES - 1:SUBLANES, c0:c0 + CONV_CHUNK])
        prev = jnp.where(seq_start, zrow, ucarry_scr[SUBLANES - 2:SUBLANES - 1, c0:c0 + CONV_CHUNK])
        u1 = jnp.where(row == 0, last, pltpu.roll(u, 1, axis=0))
        u2 = jnp.where(row == 0, prev, jnp.where(row == 1, last, pltpu.roll(u, 2, axis=0)))
        w = conv_w_ref[:, c0:c0 + CONV_CHUNK]
        y = w[0:1, :] * u2 + w[1:2, :] * u1 + w[2:3, :] * u
        ucarry_scr[:, c0:c0 + CONV_CHUNK] = u[tm - SUBLANES:tm, :]
        merged = jax.nn.sigmoid(gc) * (cb * y) + jax.nn.sigmoid(ga) * y_attn
        merged_scr[:, c0:c0 + CONV_CHUNK] = merged.astype(BF16)

    ffn = _dot(hid_scr[...], w_down_ref[...])
    o_ref[...] = _layer_norm(ALPHA * x1 + ffn, g2_ref[...], b2_ref[...])

    mix = _dot(merged_scr[...], w_out_ref[...])
    z1_scr[...] = ALPHA * x_ref[...] + mix


def _resident(shape):
    return pl.BlockSpec(shape, lambda *_: (0,) * len(shape), pipeline_mode=pl.Buffered(1))


def _block_call(x, sinks, w_in, conv_w, w_out, g1, b1, w_gate, w_up, w_down, g2, b2, seq):
    tokens = x.shape[0]
    tm = TILE
    nblk = tm // BLOCK
    n_tiles = tokens // tm
    in_tile = pl.BlockSpec((tm, D_MODEL), lambda s: (jnp.minimum(s, n_tiles - 1), 0))
    out_tile = pl.BlockSpec((tm, D_MODEL), lambda s: (jnp.maximum(s - 1, 0), 0))
    in_hbm = pl.BlockSpec(memory_space=pl.ANY)
    return pl.pallas_call(
        functools.partial(_block_kernel, tiles_per_seq=seq // tm),
        grid=(n_tiles + 1,),
        in_specs=[
            pl.BlockSpec(memory_space=pltpu.SMEM),
            in_tile,
            in_hbm,
            _resident((CONV_K, D_MODEL)),
            in_hbm,
            _resident((1, D_MODEL)),
            _resident((1, D_MODEL)),
            _resident((N_Q_HEADS, BLOCK, BLOCK)),
            in_hbm,
            in_hbm,
            in_hbm,
            _resident((1, D_MODEL)),
            _resident((1, D_MODEL)),
        ],
        out_specs=out_tile,
        out_shape=jax.ShapeDtypeStruct(x.shape, F32),
        scratch_shapes=[
            pltpu.VMEM((nblk, N_KV_HEADS, 2 * BLOCK, LANES), BF16),
            pltpu.VMEM((tm + BLOCK, 2 * N_KV_HEADS * LANES), BF16),
            pltpu.VMEM((N_KV_HEADS * 2 * HEAD_DIM, tm + BLOCK), BF16),
            pltpu.VMEM((SUBLANES, D_MODEL), F32),
            pltpu.VMEM((tm, D_MODEL), BF16),
            pltpu.VMEM((tm, D_MODEL), F32),
            pltpu.VMEM((tm, D_FF), BF16),
            pltpu.VMEM((D_MODEL, IN_WIDTH), BF16),
            pltpu.VMEM((D_MODEL, D_MODEL), BF16),
            pltpu.VMEM((D_MODEL, D_FF), BF16),
            pltpu.VMEM((D_MODEL, D_FF), BF16),
            pltpu.VMEM((D_FF, D_MODEL), BF16),
            pltpu.VMEM((2, CAST_ROWS_IN, IN_WIDTH), F32),
            pltpu.VMEM((2, CAST_ROWS_FF, D_FF), F32),
            pltpu.VMEM((2, CAST_ROWS_SQ, D_MODEL), F32),
            pltpu.SemaphoreType.DMA((2,)),
        ],
        compiler_params=pltpu.CompilerParams(
            dimension_semantics=("arbitrary",), vmem_limit_bytes=VMEM_LIMIT),
        name="block",
    )(sinks, x, w_in, conv_w, w_out, g1, b1, jnp.asarray(_alibi_bias()), w_gate, w_up, w_down, g2, b2)


def kernel(x, w_in, conv_w, attn_sinks, w_out, ln1_g, ln1_b, w_gate, w_up, w_down, ln2_g, ln2_b):
    bsz, seq, _ = x.shape
    assert seq % TILE == 0 and TILE % BLOCK == 0
    x = x.reshape(bsz * seq, D_MODEL)
    for l in range(DEPTH):
        x = _block_call(x, attn_sinks[l], w_in[l], conv_w[l], w_out[l],
                        ln1_g[l][None, :], ln1_b[l][None, :], w_gate[l], w_up[l], w_down[l],
                        ln2_g[l][None, :], ln2_b[l][None, :], seq)
    return x.reshape(bsz, seq, D_MODEL)
```

```python
import functools

import numpy as np

import jax
import jax.numpy as jnp
from jax import lax
from jax.experimental import pallas as pl
from jax.experimental.pallas import tpu as pltpu

D_MODEL = 1024
HEAD_DIM = 64
N_Q_HEADS = D_MODEL // HEAD_DIM
N_KV_HEADS = N_Q_HEADS // 4
GQA_GROUP = N_Q_HEADS // N_KV_HEADS
KV_WIDTH = N_KV_HEADS * HEAD_DIM
WINDOW = 128
BLOCK = 128
CONV_K = 3
D_FF = 2816
DEPTH = 1
ALPHA = (2.0 * DEPTH) ** 0.25
LN_EPS = 1e-5
NEG_INF = -1e30

OFF_Q = 0
OFF_K = OFF_Q + D_MODEL
OFF_V = OFF_K + KV_WIDTH
OFF_CB = OFF_V + KV_WIDTH
OFF_CC = OFF_CB + D_MODEL
OFF_CH = OFF_CC + D_MODEL
OFF_GC = OFF_CH + D_MODEL
OFF_GA = OFF_GC + D_MODEL
IN_WIDTH = OFF_GA + D_MODEL

LANES = 128
SUBLANES = 8
TILE = 256
CONV_CHUNK = GQA_GROUP * HEAD_DIM
FFN_CHUNK = 256
N_FFN_CHUNKS = D_FF // FFN_CHUNK
FFN_HEAD_CHUNKS = N_FFN_CHUNKS - 2 * N_KV_HEADS
VMEM_LIMIT = 58 * 1024 * 1024
CAST_SLOTS = 3
CAST_ROWS_IN = 64
CAST_ROWS_FF = 128
CAST_ROWS_SQ = 256

F32 = jnp.float32
BF16 = jnp.bfloat16


def _alibi_bias():
    slopes = np.array([2.0 ** (-8.0 * (h + 1) / N_Q_HEADS) for h in range(N_Q_HEADS)], dtype=np.float32)
    kj = np.arange(BLOCK)[:, None]
    qi = np.arange(BLOCK)[None, :]
    dist = ((qi - kj) % BLOCK).astype(np.float32)
    return -slopes[:, None, None] * dist[None]


def _dot(a, b):
    return jnp.dot(a, b, preferred_element_type=F32)


def _dot_nt(a, b):
    return lax.dot_general(a, b, (((1,), (1,)), ((), ())), preferred_element_type=F32)


def _layer_norm(z, g, b):
    mu = jnp.mean(z, axis=-1, keepdims=True)
    d = z - mu
    var = jnp.mean(d * d, axis=-1, keepdims=True)
    return d * lax.rsqrt(var + LN_EPS) * g + b


def _cast_weight(src_hbm, dst_scr, stage, sem):
    slots, rows = stage.shape[0], stage.shape[1]
    n = src_hbm.shape[0] // rows
    assert n * rows == src_hbm.shape[0] and stage.shape[2] == src_hbm.shape[1] and n >= slots

    def chunk_copy(k):
        slot = lax.rem(k, slots)
        return pltpu.make_async_copy(src_hbm.at[pl.ds(k * rows, rows), :], stage.at[slot], sem.at[slot])

    for k in range(slots - 1):
        chunk_copy(k).start()

    def body(k, carry):
        @pl.when(k + slots - 1 < n)
        def _():
            chunk_copy(k + slots - 1).start()

        chunk_copy(k).wait()
        dst_scr[pl.ds(pl.multiple_of(k * rows, rows), rows), :] = stage[lax.rem(k, slots)].astype(BF16)
        return carry

    lax.fori_loop(0, n, body, 0)


def _block_kernel(sinks_ref, x_ref, w_in_hbm, conv_w_ref, w_out_hbm, g1_ref, b1_ref, bias_ref,
                  w_gate_hbm, w_up_hbm, w_down_hbm, g2_ref, b2_ref, o_ref,
                  q_scr, k_scr, vt_scr, ucarry_scr, merged_scr, z1_scr, hid_scr,
                  w_in_ref, w_out_ref, w_gate_ref, w_up_ref, w_down_ref, stage_in, stage_ff, stage_sq, cast_sem,
                  *, tiles_per_seq):
    tm = x_ref.shape[0]
    nblk = tm // BLOCK
    s = pl.program_id(0)
    i = lax.rem(s, tiles_per_seq)
    seq_start = i == 0

    @pl.when(s == 0)
    def _():
        z1_scr[...] = jnp.zeros(z1_scr.shape, F32)
        k_scr[...] = jnp.zeros(k_scr.shape, BF16)
        vt_scr[...] = jnp.zeros(vt_scr.shape, BF16)
        ucarry_scr[...] = jnp.zeros(ucarry_scr.shape, F32)
        _cast_weight(w_in_hbm, w_in_ref, stage_in, cast_sem)
        _cast_weight(w_out_hbm, w_out_ref, stage_sq, cast_sem)
        _cast_weight(w_gate_hbm, w_gate_ref, stage_ff, cast_sem)
        _cast_weight(w_up_hbm, w_up_ref, stage_ff, cast_sem)
        _cast_weight(w_down_hbm, w_down_ref, stage_sq, cast_sem)

    x1 = _layer_norm(z1_scr[...], g1_ref[...], b1_ref[...])
    x1b = x1.astype(BF16)

    def gate_up(c):
        c0 = c * FFN_CHUNK
        gate = _dot(x1b, w_gate_ref[:, c0:c0 + FFN_CHUNK])
        up = _dot(x1b, w_up_ref[:, c0:c0 + FFN_CHUNK])
        hid_scr[:, c0:c0 + FFN_CHUNK] = (gate * jax.nn.sigmoid(gate) * up).astype(BF16)

    k_scr[0:BLOCK, :] = jnp.where(seq_start, jnp.zeros((BLOCK, k_scr.shape[1]), BF16), k_scr[tm:tm + BLOCK, :])
    vt_scr[:, 0:BLOCK] = jnp.where(seq_start, jnp.zeros((vt_scr.shape[0], BLOCK), BF16), vt_scr[:, tm:tm + BLOCK])

    xb = x_ref[...].astype(BF16)
    lane = lax.broadcasted_iota(jnp.int32, (tm, LANES), 1)
    lo = lane < HEAD_DIM

    qkv = _dot(xb, w_in_ref[:, OFF_Q:OFF_CB])
    for blk in range(nblk):
        for pr in range(N_Q_HEADS // 2):
            qp = qkv[blk * BLOCK:(blk + 1) * BLOCK, pr * LANES:(pr + 1) * LANES] * (HEAD_DIM ** -0.5)
            q_scr[blk, pr // 2, (pr % 2) * BLOCK:(pr % 2 + 1) * BLOCK, :] = qp.astype(BF16)

    for t in range(KV_WIDTH // LANES):
        kt = qkv[:, OFF_K + t * LANES:OFF_K + (t + 1) * LANES]
        kr = pltpu.roll(kt, HEAD_DIM, axis=1)
        zero = jnp.zeros_like(kt)
        h0, h1 = 2 * t, 2 * t + 1
        k_scr[BLOCK:, (2 * h0) * LANES:(2 * h0 + 1) * LANES] = jnp.where(lo, kt, zero).astype(BF16)
        k_scr[BLOCK:, (2 * h0 + 1) * LANES:(2 * h0 + 2) * LANES] = jnp.where(lo, zero, kr).astype(BF16)
        k_scr[BLOCK:, (2 * h1) * LANES:(2 * h1 + 1) * LANES] = jnp.where(lo, kr, zero).astype(BF16)
        k_scr[BLOCK:, (2 * h1 + 1) * LANES:(2 * h1 + 2) * LANES] = jnp.where(lo, zero, kt).astype(BF16)
    v_t = qkv[:, OFF_V:OFF_V + KV_WIDTH].T
    for h in range(N_KV_HEADS):
        vh = v_t[h * HEAD_DIM:(h + 1) * HEAD_DIM, :].astype(BF16)
        vt_scr[(2 * h) * HEAD_DIM:(2 * h + 1) * HEAD_DIM, BLOCK:] = vh
        vt_scr[(2 * h + 1) * HEAD_DIM:(2 * h + 2) * HEAD_DIM, BLOCK:] = vh

    for c in range(FFN_HEAD_CHUNKS):
        gate_up(c)

    kj = lax.broadcasted_iota(jnp.int32, (BLOCK, BLOCK), 0)
    qi = lax.broadcasted_iota(jnp.int32, (BLOCK, BLOCK), 1)
    tri = kj <= qi
    top = kj < HEAD_DIM
    valid0 = (kj - qi) <= jnp.where(seq_start, 0, BLOCK)
    row = lax.broadcasted_iota(jnp.int32, (tm, CONV_CHUNK), 0)

    def scores(blk, h):
        r0 = blk * BLOCK
        qg = q_scr[blk, h]
        ke = k_scr[r0:r0 + 2 * BLOCK, (2 * h) * LANES:(2 * h + 1) * LANES]
        ko = k_scr[r0:r0 + 2 * BLOCK, (2 * h + 1) * LANES:(2 * h + 2) * LANES]
        se = _dot_nt(ke, qg)
        so = _dot_nt(ko, qg)
        return se, so

    def attend(blk, h, se, so):
        r0 = blk * BLOCK
        probs, inv = [], []
        for g, (sc, half) in enumerate(((se, 0), (so, 0), (se, 1), (so, 1))):
            head = GQA_GROUP * h + g
            s_prev = sc[0:BLOCK, half * BLOCK:(half + 1) * BLOCK]
            s_cur = sc[BLOCK:2 * BLOCK, half * BLOCK:(half + 1) * BLOCK]
            sm = jnp.where(tri, s_cur, s_prev) + bias_ref[head]
            if blk == 0:
                sm = jnp.where(valid0, sm, NEG_INF)
            sink = sinks_ref[head]
            m = jnp.maximum(jnp.max(sm, axis=0, keepdims=True), sink)
            p = jnp.exp(sm - m)
            denom = jnp.sum(p, axis=0, keepdims=True) + jnp.exp(sink - m)
            inv.append(1.0 / denom)
            zero = jnp.zeros_like(p)
            probs.append(jnp.concatenate([jnp.where(tri, zero, p), jnp.where(tri, p, zero)],
                                         axis=0).astype(BF16))
        p_t = jnp.concatenate(probs, axis=1)
        vt = vt_scr[(2 * h) * HEAD_DIM:(2 * h + 2) * HEAD_DIM, r0:r0 + 2 * BLOCK]
        o_t = _dot(vt, p_t) * jnp.concatenate(inv, axis=1)
        pairs = []
        for pr in range(2):
            oe = o_t[:, (2 * pr) * BLOCK:(2 * pr + 1) * BLOCK]
            oo = o_t[:, (2 * pr + 1) * BLOCK:(2 * pr + 2) * BLOCK]
            pairs.append(jnp.where(top, oe, oo).T)
        return jnp.concatenate(pairs, axis=1)

    def proj(off, h):
        c0 = off + h * CONV_CHUNK
        return _dot(xb, w_in_ref[:, c0:c0 + CONV_CHUNK])

    units = [(h, blk) for h in range(N_KV_HEADS) for blk in range(nblk)]
    assert nblk == 2
    sc = scores(0, 0)
    for n, (h, blk) in enumerate(units):
        c0 = h * CONV_CHUNK
        if blk == 0:
            cb, cc, ch = proj(OFF_CB, h), proj(OFF_CC, h), proj(OFF_CH, h)
            y_blocks = []
        else:
            gc, ga = proj(OFF_GC, h), proj(OFF_GA, h)
        gate_up(FFN_HEAD_CHUNKS + n)
        if n + 1 < len(units):
            sc_next = scores(units[n + 1][1], units[n + 1][0])
        y_blocks.append(attend(blk, h, *sc))
        sc = sc_next
        if blk == 0:
            continue
        y_attn = jnp.concatenate(y_blocks, axis=0)
        u = cc * ch
        zrow = jnp.zeros((1, CONV_CHUNK), F32)
        last = jnp.where(seq_start, zrow, ucarry_scr[SUBLANES - 1:SUBLANES, c0:c0 + CONV_CHUNK])
        prev = jnp.where(seq_start, zrow, ucarry_scr[SUBLANES - 2:SUBLANES - 1, c0:c0 + CONV_CHUNK])
        u1 = jnp.where(row == 0, last, pltpu.roll(u, 1, axis=0))
        u2 = jnp.where(row == 0, prev, jnp.where(row == 1, last, pltpu.roll(u, 2, axis=0)))
        w = conv_w_ref[:, c0:c0 + CONV_CHUNK]
        y = w[0:1, :] * u2 + w[1:2, :] * u1 + w[2:3, :] * u
        ucarry_scr[:, c0:c0 + CONV_CHUNK] = u[tm - SUBLANES:tm, :]
        merged = jax.nn.sigmoid(gc) * (cb * y) + jax.nn.sigmoid(ga) * y_attn
        merged_scr[:, c0:c0 + CONV_CHUNK] = merged.astype(BF16)

    ffn = _dot(hid_scr[...], w_down_ref[...])
    o_ref[...] = _layer_norm(ALPHA * x1 + ffn, g2_ref[...], b2_ref[...])

    mix = _dot(merged_scr[...], w_out_ref[...])
    z1_scr[...] = ALPHA * x_ref[...] + mix


def _resident(shape):
    return pl.BlockSpec(shape, lambda *_: (0,) * len(shape), pipeline_mode=pl.Buffered(1))


def _block_call(x, sinks, w_in, conv_w, w_out, g1, b1, w_gate, w_up, w_down, g2, b2, seq):
    tokens = x.shape[0]
    tm = TILE
    nblk = tm // BLOCK
    n_tiles = tokens // tm
    in_tile = pl.BlockSpec((tm, D_MODEL), lambda s: (jnp.minimum(s, n_tiles - 1), 0))
    out_tile = pl.BlockSpec((tm, D_MODEL), lambda s: (jnp.maximum(s - 1, 0), 0))
    in_hbm = pl.BlockSpec(memory_space=pl.ANY)
    return pl.pallas_call(
        functools.partial(_block_kernel, tiles_per_seq=seq // tm),
        grid=(n_tiles + 1,),
        in_specs=[
            pl.BlockSpec(memory_space=pltpu.SMEM),
            in_tile,
            in_hbm,
            _resident((CONV_K, D_MODEL)),
            in_hbm,
            _resident((1, D_MODEL)),
            _resident((1, D_MODEL)),
            _resident((N_Q_HEADS, BLOCK, BLOCK)),
            in_hbm,
            in_hbm,
            in_hbm,
            _resident((1, D_MODEL)),
            _resident((1, D_MODEL)),
        ],
        out_specs=out_tile,
        out_shape=jax.ShapeDtypeStruct(x.shape, F32),
        scratch_shapes=[
            pltpu.VMEM((nblk, N_KV_HEADS, 2 * BLOCK, LANES), BF16),
            pltpu.VMEM((tm + BLOCK, 2 * N_KV_HEADS * LANES), BF16),
            pltpu.VMEM((N_KV_HEADS * 2 * HEAD_DIM, tm + BLOCK), BF16),
            pltpu.VMEM((SUBLANES, D_MODEL), F32),
            pltpu.VMEM((tm, D_MODEL), BF16),
            pltpu.VMEM((tm, D_MODEL), F32),
            pltpu.VMEM((tm, D_FF), BF16),
            pltpu.VMEM((D_MODEL, IN_WIDTH), BF16),
            pltpu.VMEM((D_MODEL, D_MODEL), BF16),
            pltpu.VMEM((D_MODEL, D_FF), BF16),
            pltpu.VMEM((D_MODEL, D_FF), BF16),
            pltpu.VMEM((D_FF, D_MODEL), BF16),
            pltpu.VMEM((CAST_SLOTS, CAST_ROWS_IN, IN_WIDTH), F32),
            pltpu.VMEM((CAST_SLOTS, CAST_ROWS_FF, D_FF), F32),
            pltpu.VMEM((CAST_SLOTS, CAST_ROWS_SQ, D_MODEL), F32),
            pltpu.SemaphoreType.DMA((CAST_SLOTS,)),
        ],
        compiler_params=pltpu.CompilerParams(
            dimension_semantics=("arbitrary",), vmem_limit_bytes=VMEM_LIMIT),
        name="block",
    )(sinks, x, w_in, conv_w, w_out, g1, b1, jnp.asarray(_alibi_bias()), w_gate, w_up, w_down, g2, b2)


def kernel(x, w_in, conv_w, attn_sinks, w_out, ln1_g, ln1_b, w_gate, w_up, w_down, ln2_g, ln2_b):
    bsz, seq, _ = x.shape
    assert seq % TILE == 0 and TILE % BLOCK == 0
    x = x.reshape(bsz * seq, D_MODEL)
    for l in range(DEPTH):
        x = _block_call(x, attn_sinks[l], w_in[l], conv_w[l], w_out[l],
                        ln1_g[l][None, :], ln1_b[l][None, :], w_gate[l], w_up[l], w_down[l],
                        ln2_g[l][None, :], ln2_b[l][None, :], seq)
    return x.reshape(bsz, seq, D_MODEL)
```

```python
import functools

import numpy as np

import jax
import jax.numpy as jnp
from jax import lax
from jax.experimental import pallas as pl
from jax.experimental.pallas import tpu as pltpu

D_MODEL = 1024
HEAD_DIM = 64
N_Q_HEADS = D_MODEL // HEAD_DIM
N_KV_HEADS = N_Q_HEADS // 4
GQA_GROUP = N_Q_HEADS // N_KV_HEADS
KV_WIDTH = N_KV_HEADS * HEAD_DIM
WINDOW = 128
BLOCK = 128
CONV_K = 3
D_FF = 2816
DEPTH = 1
ALPHA = (2.0 * DEPTH) ** 0.25
LN_EPS = 1e-5
NEG_INF = -1e30

OFF_Q = 0
OFF_K = OFF_Q + D_MODEL
OFF_V = OFF_K + KV_WIDTH
OFF_CB = OFF_V + KV_WIDTH
OFF_CC = OFF_CB + D_MODEL
OFF_CH = OFF_CC + D_MODEL
OFF_GC = OFF_CH + D_MODEL
OFF_GA = OFF_GC + D_MODEL
IN_WIDTH = OFF_GA + D_MODEL

LANES = 128
SUBLANES = 8
TILE = 256
CONV_CHUNK = GQA_GROUP * HEAD_DIM
FFN_CHUNK = 256
N_FFN_CHUNKS = D_FF // FFN_CHUNK
FFN_HEAD_CHUNKS = N_FFN_CHUNKS - 2 * N_KV_HEADS
VMEM_LIMIT = 58 * 1024 * 1024
CAST_SLOTS = 3
CAST_ROWS_IN = 64
CAST_ROWS_FF = 128
CAST_ROWS_SQ = 256

F32 = jnp.float32
BF16 = jnp.bfloat16


def _alibi_bias():
    slopes = np.array([2.0 ** (-8.0 * (h + 1) / N_Q_HEADS) for h in range(N_Q_HEADS)], dtype=np.float32)
    kj = np.arange(BLOCK)[:, None]
    qi = np.arange(BLOCK)[None, :]
    dist = ((qi - kj) % BLOCK).astype(np.float32)
    return -slopes[:, None, None] * dist[None]


def _dot(a, b):
    return jnp.dot(a, b, preferred_element_type=F32)


def _dot_nt(a, b):
    return lax.dot_general(a, b, (((1,), (1,)), ((), ())), preferred_element_type=F32)


def _layer_norm(z, g, b):
    mu = jnp.mean(z, axis=-1, keepdims=True)
    d = z - mu
    var = jnp.mean(d * d, axis=-1, keepdims=True)
    return d * lax.rsqrt(var + LN_EPS) * g + b


def _cast_weight(src_hbm, dst_scr, stage, sem):
    slots, rows = stage.shape[0], stage.shape[1]
    n = src_hbm.shape[0] // rows
    assert n * rows == src_hbm.shape[0] and stage.shape[2] == src_hbm.shape[1] and n >= slots

    def chunk_copy(k):
        slot = lax.rem(k, slots)
        return pltpu.make_async_copy(src_hbm.at[pl.ds(k * rows, rows), :], stage.at[slot], sem.at[slot])

    for k in range(slots - 1):
        chunk_copy(k).start()

    def body(k, carry):
        @pl.when(k + slots - 1 < n)
        def _():
            chunk_copy(k + slots - 1).start()

        chunk_copy(k).wait()
        dst_scr[pl.ds(pl.multiple_of(k * rows, rows), rows), :] = stage[lax.rem(k, slots)].astype(BF16)
        return carry

    lax.fori_loop(0, n, body, 0)


def _block_kernel(sinks_ref, x_ref, w_in_hbm, conv_w_ref, w_out_hbm, g1_ref, b1_ref, bias_ref,
                  w_gate_hbm, w_up_hbm, w_down_hbm, g2_ref, b2_ref, o_ref,
                  q_scr, k_scr, vt_scr, ucarry_scr, merged_scr, z1_scr, hid_scr,
                  w_in_ref, w_out_ref, w_gate_ref, w_up_ref, w_down_ref, stage_in, stage_ff, stage_sq, cast_sem,
                  *, tiles_per_seq):
    tm = x_ref.shape[0]
    nblk = tm // BLOCK
    s = pl.program_id(0)
    i = lax.rem(s, tiles_per_seq)
    seq_start = i == 0

    @pl.when(s == 0)
    def _():
        z1_scr[...] = jnp.zeros(z1_scr.shape, F32)
        k_scr[...] = jnp.zeros(k_scr.shape, BF16)
        vt_scr[...] = jnp.zeros(vt_scr.shape, BF16)
        ucarry_scr[...] = jnp.zeros(ucarry_scr.shape, F32)
        _cast_weight(w_in_hbm, w_in_ref, stage_in, cast_sem)
        _cast_weight(w_out_hbm, w_out_ref, stage_sq, cast_sem)
        _cast_weight(w_gate_hbm, w_gate_ref, stage_ff, cast_sem)
        _cast_weight(w_up_hbm, w_up_ref, stage_ff, cast_sem)
        _cast_weight(w_down_hbm, w_down_ref, stage_sq, cast_sem)

    x1 = _layer_norm(z1_scr[...], g1_ref[...], b1_ref[...])
    x1b = x1.astype(BF16)

    def gate_up(c):
        c0 = c * FFN_CHUNK
        gate = _dot(x1b, w_gate_ref[:, c0:c0 + FFN_CHUNK])
        up = _dot(x1b, w_up_ref[:, c0:c0 + FFN_CHUNK])
        hid_scr[:, c0:c0 + FFN_CHUNK] = (gate * jax.nn.sigmoid(gate) * up).astype(BF16)

    k_scr[0:BLOCK, :] = jnp.where(seq_start, jnp.zeros((BLOCK, k_scr.shape[1]), BF16), k_scr[tm:tm + BLOCK, :])
    vt_scr[:, 0:BLOCK] = jnp.where(seq_start, jnp.zeros((vt_scr.shape[0], BLOCK), BF16), vt_scr[:, tm:tm + BLOCK])

    xb = x_ref[...].astype(BF16)
    lane = lax.broadcasted_iota(jnp.int32, (tm, LANES), 1)
    lo = lane < HEAD_DIM

    qkv = _dot(xb, w_in_ref[:, OFF_Q:OFF_CB])
    for blk in range(nblk):
        for pr in range(N_Q_HEADS // 2):
            qp = qkv[blk * BLOCK:(blk + 1) * BLOCK, pr * LANES:(pr + 1) * LANES] * (HEAD_DIM ** -0.5)
            q_scr[blk, pr // 2, (pr % 2) * BLOCK:(pr % 2 + 1) * BLOCK, :] = qp.astype(BF16)

    for t in range(KV_WIDTH // LANES):
        kt = qkv[:, OFF_K + t * LANES:OFF_K + (t + 1) * LANES]
        kr = pltpu.roll(kt, HEAD_DIM, axis=1)
        zero = jnp.zeros_like(kt)
        h0, h1 = 2 * t, 2 * t + 1
        k_scr[BLOCK:, (2 * h0) * LANES:(2 * h0 + 1) * LANES] = jnp.where(lo, kt, zero).astype(BF16)
        k_scr[BLOCK:, (2 * h0 + 1) * LANES:(2 * h0 + 2) * LANES] = jnp.where(lo, zero, kr).astype(BF16)
        k_scr[BLOCK:, (2 * h1) * LANES:(2 * h1 + 1) * LANES] = jnp.where(lo, kr, zero).astype(BF16)
        k_scr[BLOCK:, (2 * h1 + 1) * LANES:(2 * h1 + 2) * LANES] = jnp.where(lo, zero, kt).astype(BF16)
    v_t = qkv[:, OFF_V:OFF_V + KV_WIDTH].T
    for h in range(N_KV_HEADS):
        vh = v_t[h * HEAD_DIM:(h + 1) * HEAD_DIM, :].astype(BF16)
        vt_scr[(2 * h) * HEAD_DIM:(2 * h + 1) * HEAD_DIM, BLOCK:] = vh
        vt_scr[(2 * h + 1) * HEAD_DIM:(2 * h + 2) * HEAD_DIM, BLOCK:] = vh

    for c in range(FFN_HEAD_CHUNKS):
        gate_up(c)

    kj = lax.broadcasted_iota(jnp.int32, (BLOCK, BLOCK), 0)
    qi = lax.broadcasted_iota(jnp.int32, (BLOCK, BLOCK), 1)
    tri = kj <= qi
    top = kj < HEAD_DIM
    valid0 = (kj - qi) <= jnp.where(seq_start, 0, BLOCK)
    row = lax.broadcasted_iota(jnp.int32, (tm, CONV_CHUNK), 0)

    def scores(blk, h):
        r0 = blk * BLOCK
        qg = q_scr[blk, h]
        ke = k_scr[r0:r0 + 2 * BLOCK, (2 * h) * LANES:(2 * h + 1) * LANES]
        ko = k_scr[r0:r0 + 2 * BLOCK, (2 * h + 1) * LANES:(2 * h + 2) * LANES]
        se = _dot_nt(ke, qg)
        so = _dot_nt(ko, qg)
        return se, so

    def attend(blk, h, se, so):
        r0 = blk * BLOCK
        probs, inv = [], []
        for g, (sc, half) in enumerate(((se, 0), (so, 0), (se, 1), (so, 1))):
            head = GQA_GROUP * h + g
            s_prev = sc[0:BLOCK, half * BLOCK:(half + 1) * BLOCK]
            s_cur = sc[BLOCK:2 * BLOCK, half * BLOCK:(half + 1) * BLOCK]
            sm = jnp.where(tri, s_cur, s_prev) + bias_ref[head]
            if blk == 0:
                sm = jnp.where(valid0, sm, NEG_INF)
            sink = sinks_ref[head]
            m = jnp.maximum(jnp.max(sm, axis=0, keepdims=True), sink)
            p = jnp.exp(sm - m)
            denom = jnp.sum(p, axis=0, keepdims=True) + jnp.exp(sink - m)
            inv.append(1.0 / denom)
            zero = jnp.zeros_like(p)
            probs.append(jnp.concatenate([jnp.where(tri, zero, p), jnp.where(tri, p, zero)],
                                         axis=0).astype(BF16))
        p_t = jnp.concatenate(probs, axis=1)
        vt = vt_scr[(2 * h) * HEAD_DIM:(2 * h + 2) * HEAD_DIM, r0:r0 + 2 * BLOCK]
        o_t = _dot(vt, p_t) * jnp.concatenate(inv, axis=1)
        pairs = []
        for pr in range(2):
            oe = o_t[:, (2 * pr) * BLOCK:(2 * pr + 1) * BLOCK]
            oo = o_t[:, (2 * pr + 1) * BLOCK:(2 * pr + 2) * BLOCK]
            pairs.append(jnp.where(top, oe, oo).T)
        return jnp.concatenate(pairs, axis=1)

    def proj(off, h):
        c0 = off + h * CONV_CHUNK
        return _dot(xb, w_in_ref[:, c0:c0 + CONV_CHUNK])

    units = [(h, blk) for h in range(N_KV_HEADS) for blk in range(nblk)]
    assert nblk == 2
    sc = scores(0, 0)
    for n, (h, blk) in enumerate(units):
        c0 = h * CONV_CHUNK
        if blk == 0:
            cb, cc, ch = proj(OFF_CB, h), proj(OFF_CC, h), proj(OFF_CH, h)
            y_blocks = []
        else:
            gc, ga = proj(OFF_GC, h), proj(OFF_GA, h)
        gate_up(FFN_HEAD_CHUNKS + n)
        if n + 1 < len(units):
            sc_next = scores(units[n + 1][1], units[n + 1][0])
        y_blocks.append(attend(blk, h, *sc))
        sc = sc_next
        if blk == 0:
            continue
        y_attn = jnp.concatenate(y_blocks, axis=0)
        u = cc * ch
        zrow = jnp.zeros((1, CONV_CHUNK), F32)
        last = jnp.where(seq_start, zrow, ucarry_scr[SUBLANES - 1:SUBLANES, c0:c0 + CONV_CHUNK])
        prev = jnp.where(seq_start, zrow, ucarry_scr[SUBLANES - 2:SUBLANES - 1, c0:c0 + CONV_CHUNK])
        u1 = jnp.where(row == 0, last, pltpu.roll(u, 1, axis=0))
        u2 = jnp.where(row == 0, prev, jnp.where(row == 1, last, pltpu.roll(u, 2, axis=0)))
        w = conv_w_ref[:, c0:c0 + CONV_CHUNK]
        y = w[0:1, :] * u2 + w[1:2, :] * u1 + w[2:3, :] * u
        ucarry_scr[:, c0:c0 + CONV_CHUNK] = u[tm - SUBLANES:tm, :]
        merged = jax.nn.sigmoid(gc) * (cb * y) + jax.nn.sigmoid(ga) * y_attn
        merged_scr[:, c0:c0 + CONV_CHUNK] = merged.astype(BF16)

    ffn_a = _dot(hid_scr[0:BLOCK, :], w_down_ref[...])
    ya = _layer_norm(ALPHA * x1[0:BLOCK, :] + ffn_a, g2_ref[...], b2_ref[...])
    o_ref[0:BLOCK, :] = ya
    bits = lax.bitcast_convert_type(ya, jnp.uint32)
    folded = bits[:, 0:LANES]
    for c in range(1, D_MODEL // LANES):
        folded = folded | bits[:, c * LANES:(c + 1) * LANES]
    plus_zero = lax.bitcast_convert_type((folded >> 16) >> 16, F32)
    ffn_b = _dot(hid_scr[BLOCK:2 * BLOCK, :], w_down_ref[...])
    zb = ALPHA * x1[BLOCK:2 * BLOCK, :] + ffn_b
    zb = jnp.concatenate([zb[:, 0:LANES] + plus_zero, zb[:, LANES:]], axis=1)
    o_ref[BLOCK:2 * BLOCK, :] = _layer_norm(zb, g2_ref[...], b2_ref[...])

    mix = _dot(merged_scr[...], w_out_ref[...])
    z1_scr[...] = ALPHA * x_ref[...] + mix


def _resident(shape):
    return pl.BlockSpec(shape, lambda *_: (0,) * len(shape), pipeline_mode=pl.Buffered(1))


def _block_call(x, sinks, w_in, conv_w, w_out, g1, b1, w_gate, w_up, w_down, g2, b2, seq):
    tokens = x.shape[0]
    tm = TILE
    nblk = tm // BLOCK
    n_tiles = tokens // tm
    in_tile = pl.BlockSpec((tm, D_MODEL), lambda s: (jnp.minimum(s, n_tiles - 1), 0))
    out_tile = pl.BlockSpec((tm, D_MODEL), lambda s: (jnp.maximum(s - 1, 0), 0))
    in_hbm = pl.BlockSpec(memory_space=pl.ANY)
    return pl.pallas_call(
        functools.partial(_block_kernel, tiles_per_seq=seq // tm),
        grid=(n_tiles + 1,),
        in_specs=[
            pl.BlockSpec(memory_space=pltpu.SMEM),
            in_tile,
            in_hbm,
            _resident((CONV_K, D_MODEL)),
            in_hbm,
            _resident((1, D_MODEL)),
            _resident((1, D_MODEL)),
            _resident((N_Q_HEADS, BLOCK, BLOCK)),
            in_hbm,
            in_hbm,
            in_hbm,
            _resident((1, D_MODEL)),
            _resident((1, D_MODEL)),
        ],
        out_specs=out_tile,
        out_shape=jax.ShapeDtypeStruct(x.shape, F32),
        scratch_shapes=[
            pltpu.VMEM((nblk, N_KV_HEADS, 2 * BLOCK, LANES), BF16),
            pltpu.VMEM((tm + BLOCK, 2 * N_KV_HEADS * LANES), BF16),
            pltpu.VMEM((N_KV_HEADS * 2 * HEAD_DIM, tm + BLOCK), BF16),
            pltpu.VMEM((SUBLANES, D_MODEL), F32),
            pltpu.VMEM((tm, D_MODEL), BF16),
            pltpu.VMEM((tm, D_MODEL), F32),
            pltpu.VMEM((tm, D_FF), BF16),
            pltpu.VMEM((D_MODEL, IN_WIDTH), BF16),
            pltpu.VMEM((D_MODEL, D_MODEL), BF16),
            pltpu.VMEM((D_MODEL, D_FF), BF16),
            pltpu.VMEM((D_MODEL, D_FF), BF16),
            pltpu.VMEM((D_FF, D_MODEL), BF16),
            pltpu.VMEM((CAST_SLOTS, CAST_ROWS_IN, IN_WIDTH), F32),
            pltpu.VMEM((CAST_SLOTS, CAST_ROWS_FF, D_FF), F32),
            pltpu.VMEM((CAST_SLOTS, CAST_ROWS_SQ, D_MODEL), F32),
            pltpu.SemaphoreType.DMA((CAST_SLOTS,)),
        ],
        compiler_params=pltpu.CompilerParams(
            dimension_semantics=("arbitrary",), vmem_limit_bytes=VMEM_LIMIT),
        name="block",
    )(sinks, x, w_in, conv_w, w_out, g1, b1, jnp.asarray(_alibi_bias()), w_gate, w_up, w_down, g2, b2)


def kernel(x, w_in, conv_w, attn_sinks, w_out, ln1_g, ln1_b, w_gate, w_up, w_down, ln2_g, ln2_b):
    bsz, seq, _ = x.shape
    assert seq % TILE == 0 and TILE % BLOCK == 0
    x = x.reshape(bsz * seq, D_MODEL)
    for l in range(DEPTH):
        x = _block_call(x, attn_sinks[l], w_in[l], conv_w[l], w_out[l],
                        ln1_g[l][None, :], ln1_b[l][None, :], w_gate[l], w_up[l], w_down[l],
                        ln2_g[l][None, :], ln2_b[l][None, :], seq)
    return x.reshape(bsz, seq, D_MODEL)
```

```python
import functools

import numpy as np

import jax
import jax.numpy as jnp
from jax import lax
from jax.experimental import pallas as pl
from jax.experimental.pallas import tpu as pltpu

D_MODEL = 1024
HEAD_DIM = 64
N_Q_HEADS = D_MODEL // HEAD_DIM
N_KV_HEADS = N_Q_HEADS // 4
GQA_GROUP = N_Q_HEADS // N_KV_HEADS
KV_WIDTH = N_KV_HEADS * HEAD_DIM
WINDOW = 128
BLOCK = 128
CONV_K = 3
D_FF = 2816
DEPTH = 1
ALPHA = (2.0 * DEPTH) ** 0.25
LN_EPS = 1e-5
NEG_INF = -1e30

OFF_Q = 0
OFF_K = OFF_Q + D_MODEL
OFF_V = OFF_K + KV_WIDTH
OFF_CB = OFF_V + KV_WIDTH
OFF_CC = OFF_CB + D_MODEL
OFF_CH = OFF_CC + D_MODEL
OFF_GC = OFF_CH + D_MODEL
OFF_GA = OFF_GC + D_MODEL
IN_WIDTH = OFF_GA + D_MODEL

LANES = 128
SUBLANES = 8
TILE = 256
CONV_CHUNK = GQA_GROUP * HEAD_DIM
FFN_CHUNK = 256
N_FFN_CHUNKS = D_FF // FFN_CHUNK
FFN_HEAD_CHUNKS = N_FFN_CHUNKS - 2 * N_KV_HEADS
VMEM_LIMIT = 58 * 1024 * 1024
CAST_SLOTS = 3
CAST_ROWS_IN = 64
CAST_ROWS_FF = 128
CAST_ROWS_SQ = 256

F32 = jnp.float32
BF16 = jnp.bfloat16


def _alibi_bias():
    slopes = np.array([2.0 ** (-8.0 * (h + 1) / N_Q_HEADS) for h in range(N_Q_HEADS)], dtype=np.float32)
    kj = np.arange(BLOCK)[:, None]
    qi = np.arange(BLOCK)[None, :]
    dist = ((qi - kj) % BLOCK).astype(np.float32)
    return -slopes[:, None, None] * dist[None]


def _dot(a, b):
    return jnp.dot(a, b, preferred_element_type=F32)


def _dot_nt(a, b):
    return lax.dot_general(a, b, (((1,), (1,)), ((), ())), preferred_element_type=F32)


def _layer_norm(z, g, b):
    mu = jnp.mean(z, axis=-1, keepdims=True)
    d = z - mu
    var = jnp.mean(d * d, axis=-1, keepdims=True)
    return d * lax.rsqrt(var + LN_EPS) * g + b


def _cast_weight(src_hbm, dst_scr, stage, sem):
    slots, rows = stage.shape[0], stage.shape[1]
    n = src_hbm.shape[0] // rows
    assert n * rows == src_hbm.shape[0] and stage.shape[2] == src_hbm.shape[1] and n >= slots

    def chunk_copy(k):
        slot = lax.rem(k, slots)
        return pltpu.make_async_copy(src_hbm.at[pl.ds(k * rows, rows), :], stage.at[slot], sem.at[slot])

    for k in range(slots - 1):
        chunk_copy(k).start()

    def body(k, carry):
        @pl.when(k + slots - 1 < n)
        def _():
            chunk_copy(k + slots - 1).start()

        chunk_copy(k).wait()
        dst_scr[pl.ds(pl.multiple_of(k * rows, rows), rows), :] = stage[lax.rem(k, slots)].astype(BF16)
        return carry

    lax.fori_loop(0, n, body, 0)


def _block_kernel(sinks_ref, x_ref, w_in_hbm, conv_w_ref, w_out_hbm, g1_ref, b1_ref, bias_ref,
                  w_gate_hbm, w_up_hbm, w_down_hbm, g2_ref, b2_ref, o_ref,
                  q_scr, k_scr, vt_scr, ucarry_scr, merged_scr, z1_scr, hid_scr,
                  w_in_ref, w_out_ref, w_gate_ref, w_up_ref, w_down_ref, stage_in, stage_ff, stage_sq, cast_sem,
                  *, tiles_per_seq):
    tm = x_ref.shape[0]
    nblk = tm // BLOCK
    s = pl.program_id(0)
    i = lax.rem(s, tiles_per_seq)
    seq_start = i == 0

    @pl.when(s == 0)
    def _():
        z1_scr[...] = jnp.zeros(z1_scr.shape, F32)
        k_scr[...] = jnp.zeros(k_scr.shape, BF16)
        vt_scr[...] = jnp.zeros(vt_scr.shape, BF16)
        ucarry_scr[...] = jnp.zeros(ucarry_scr.shape, F32)
        _cast_weight(w_in_hbm, w_in_ref, stage_in, cast_sem)
        _cast_weight(w_out_hbm, w_out_ref, stage_sq, cast_sem)
        _cast_weight(w_gate_hbm, w_gate_ref, stage_ff, cast_sem)
        _cast_weight(w_up_hbm, w_up_ref, stage_ff, cast_sem)
        _cast_weight(w_down_hbm, w_down_ref, stage_sq, cast_sem)

    x1 = _layer_norm(z1_scr[...], g1_ref[...], b1_ref[...])
    x1b = x1.astype(BF16)

    def gate_up(c):
        c0 = c * FFN_CHUNK
        gate = _dot(x1b, w_gate_ref[:, c0:c0 + FFN_CHUNK])
        up = _dot(x1b, w_up_ref[:, c0:c0 + FFN_CHUNK])
        hid_scr[:, c0:c0 + FFN_CHUNK] = (gate * jax.nn.sigmoid(gate) * up).astype(BF16)

    k_scr[0:BLOCK, :] = jnp.where(seq_start, jnp.zeros((BLOCK, k_scr.shape[1]), BF16), k_scr[tm:tm + BLOCK, :])
    vt_scr[:, 0:BLOCK] = jnp.where(seq_start, jnp.zeros((vt_scr.shape[0], BLOCK), BF16), vt_scr[:, tm:tm + BLOCK])

    xb = x_ref[...].astype(BF16)
    lane = lax.broadcasted_iota(jnp.int32, (tm, LANES), 1)
    lo = lane < HEAD_DIM

    qkv = _dot(xb, w_in_ref[:, OFF_Q:OFF_CB])
    for blk in range(nblk):
        for pr in range(N_Q_HEADS // 2):
            qp = qkv[blk * BLOCK:(blk + 1) * BLOCK, pr * LANES:(pr + 1) * LANES] * (HEAD_DIM ** -0.5)
            q_scr[blk, pr // 2, (pr % 2) * BLOCK:(pr % 2 + 1) * BLOCK, :] = qp.astype(BF16)

    for t in range(KV_WIDTH // LANES):
        kt = qkv[:, OFF_K + t * LANES:OFF_K + (t + 1) * LANES]
        kr = pltpu.roll(kt, HEAD_DIM, axis=1)
        zero = jnp.zeros_like(kt)
        h0, h1 = 2 * t, 2 * t + 1
        k_scr[BLOCK:, (2 * h0) * LANES:(2 * h0 + 1) * LANES] = jnp.where(lo, kt, zero).astype(BF16)
        k_scr[BLOCK:, (2 * h0 + 1) * LANES:(2 * h0 + 2) * LANES] = jnp.where(lo, zero, kr).astype(BF16)
        k_scr[BLOCK:, (2 * h1) * LANES:(2 * h1 + 1) * LANES] = jnp.where(lo, kr, zero).astype(BF16)
        k_scr[BLOCK:, (2 * h1 + 1) * LANES:(2 * h1 + 2) * LANES] = jnp.where(lo, zero, kt).astype(BF16)
    v_t = qkv[:, OFF_V:OFF_V + KV_WIDTH].T
    for h in range(N_KV_HEADS):
        vh = v_t[h * HEAD_DIM:(h + 1) * HEAD_DIM, :].astype(BF16)
        vt_scr[(2 * h) * HEAD_DIM:(2 * h + 1) * HEAD_DIM, BLOCK:] = vh
        vt_scr[(2 * h + 1) * HEAD_DIM:(2 * h + 2) * HEAD_DIM, BLOCK:] = vh

    for c in range(FFN_HEAD_CHUNKS):
        gate_up(c)

    kj = lax.broadcasted_iota(jnp.int32, (BLOCK, BLOCK), 0)
    qi = lax.broadcasted_iota(jnp.int32, (BLOCK, BLOCK), 1)
    tri = kj <= qi
    top = kj < HEAD_DIM
    valid0 = (kj - qi) <= jnp.where(seq_start, 0, BLOCK)

    def scores(blk, h):
        r0 = blk * BLOCK
        qg = q_scr[blk, h]
        ke = k_scr[r0:r0 + 2 * BLOCK, (2 * h) * LANES:(2 * h + 1) * LANES]
        ko = k_scr[r0:r0 + 2 * BLOCK, (2 * h + 1) * LANES:(2 * h + 2) * LANES]
        se = _dot_nt(ke, qg)
        so = _dot_nt(ko, qg)
        return se, so

    def attend(blk, h, se, so):
        r0 = blk * BLOCK
        probs, inv = [], []
        for g, (sc, half) in enumerate(((se, 0), (so, 0), (se, 1), (so, 1))):
            head = GQA_GROUP * h + g
            s_prev = sc[0:BLOCK, half * BLOCK:(half + 1) * BLOCK]
            s_cur = sc[BLOCK:2 * BLOCK, half * BLOCK:(half + 1) * BLOCK]
            sm = jnp.where(tri, s_cur, s_prev) + bias_ref[head]
            if blk == 0:
                sm = jnp.where(valid0, sm, NEG_INF)
            sink = sinks_ref[head]
            m = jnp.maximum(jnp.max(sm, axis=0, keepdims=True), sink)
            p = jnp.exp(sm - m)
            denom = jnp.sum(p, axis=0, keepdims=True) + jnp.exp(sink - m)
            inv.append(1.0 / denom)
            zero = jnp.zeros_like(p)
            probs.append(jnp.concatenate([jnp.where(tri, zero, p), jnp.where(tri, p, zero)],
                                         axis=0).astype(BF16))
        p_t = jnp.concatenate(probs, axis=1)
        vt = vt_scr[(2 * h) * HEAD_DIM:(2 * h + 2) * HEAD_DIM, r0:r0 + 2 * BLOCK]
        o_t = _dot(vt, p_t) * jnp.concatenate(inv, axis=1)
        pairs = []
        for pr in range(2):
            oe = o_t[:, (2 * pr) * BLOCK:(2 * pr + 1) * BLOCK]
            oo = o_t[:, (2 * pr + 1) * BLOCK:(2 * pr + 2) * BLOCK]
            pairs.append(jnp.where(top, oe, oo).T)
        return jnp.concatenate(pairs, axis=1)

    def proj(off, h):
        c0 = off + h * CONV_CHUNK
        return _dot(xb, w_in_ref[:, c0:c0 + CONV_CHUNK])

    units = [(h, blk) for h in range(N_KV_HEADS) for blk in range(nblk)]
    assert nblk == 2
    sc = scores(0, 0)
    for n, (h, blk) in enumerate(units):
        c0 = h * CONV_CHUNK
        if blk == 0:
            cb, cc, ch = proj(OFF_CB, h), proj(OFF_CC, h), proj(OFF_CH, h)
            y_blocks = []
        else:
            gc, ga = proj(OFF_GC, h), proj(OFF_GA, h)
        gate_up(FFN_HEAD_CHUNKS + n)
        if n + 1 < len(units):
            sc_next = scores(units[n + 1][1], units[n + 1][0])
        y_blocks.append(attend(blk, h, *sc))
        sc = sc_next
        if blk == 0:
            continue
        y_attn = jnp.concatenate(y_blocks, axis=0)
        u = cc * ch
        ucarry_scr[0:SUBLANES, c0:c0 + CONV_CHUNK] = jnp.where(
            seq_start, jnp.zeros((SUBLANES, CONV_CHUNK), F32), ucarry_scr[tm:tm + SUBLANES, c0:c0 + CONV_CHUNK])
        ucarry_scr[SUBLANES:, c0:c0 + CONV_CHUNK] = u
        u1 = ucarry_scr[SUBLANES - 1:SUBLANES - 1 + tm, c0:c0 + CONV_CHUNK]
        u2 = ucarry_scr[SUBLANES - 2:SUBLANES - 2 + tm, c0:c0 + CONV_CHUNK]
        w = conv_w_ref[:, c0:c0 + CONV_CHUNK]
        y = w[0:1, :] * u2 + w[1:2, :] * u1 + w[2:3, :] * u
        merged = jax.nn.sigmoid(gc) * (cb * y) + jax.nn.sigmoid(ga) * y_attn
        merged_scr[:, c0:c0 + CONV_CHUNK] = merged.astype(BF16)

    ffn_a = _dot(hid_scr[0:BLOCK, :], w_down_ref[...])
    ya = _layer_norm(ALPHA * x1[0:BLOCK, :] + ffn_a, g2_ref[...], b2_ref[...])
    o_ref[0:BLOCK, :] = ya
    bits = lax.bitcast_convert_type(ya, jnp.uint32)
    folded = bits[:, 0:LANES]
    for c in range(1, D_MODEL // LANES):
        folded = folded | bits[:, c * LANES:(c + 1) * LANES]
    plus_zero = lax.bitcast_convert_type((folded >> 16) >> 16, F32)
    ffn_b = _dot(hid_scr[BLOCK:2 * BLOCK, :], w_down_ref[...])
    zb = ALPHA * x1[BLOCK:2 * BLOCK, :] + ffn_b
    zb = jnp.concatenate([zb[:, 0:LANES] + plus_zero, zb[:, LANES:]], axis=1)
    o_ref[BLOCK:2 * BLOCK, :] = _layer_norm(zb, g2_ref[...], b2_ref[...])

    mix = _dot(merged_scr[...], w_out_ref[...])
    z1_scr[...] = ALPHA * x_ref[...] + mix


def _resident(shape):
    return pl.BlockSpec(shape, lambda *_: (0,) * len(shape), pipeline_mode=pl.Buffered(1))


def _block_call(x, sinks, w_in, conv_w, w_out, g1, b1, w_gate, w_up, w_down, g2, b2, seq):
    tokens = x.shape[0]
    tm = TILE
    nblk = tm // BLOCK
    n_tiles = tokens // tm
    in_tile = pl.BlockSpec((tm, D_MODEL), lambda s: (jnp.minimum(s, n_tiles - 1), 0))
    out_tile = pl.BlockSpec((tm, D_MODEL), lambda s: (jnp.maximum(s - 1, 0), 0))
    in_hbm = pl.BlockSpec(memory_space=pl.ANY)
    return pl.pallas_call(
        functools.partial(_block_kernel, tiles_per_seq=seq // tm),
        grid=(n_tiles + 1,),
        in_specs=[
            pl.BlockSpec(memory_space=pltpu.SMEM),
            in_tile,
            in_hbm,
            _resident((CONV_K, D_MODEL)),
            in_hbm,
            _resident((1, D_MODEL)),
            _resident((1, D_MODEL)),
            _resident((N_Q_HEADS, BLOCK, BLOCK)),
            in_hbm,
            in_hbm,
            in_hbm,
            _resident((1, D_MODEL)),
            _resident((1, D_MODEL)),
        ],
        out_specs=out_tile,
        out_shape=jax.ShapeDtypeStruct(x.shape, F32),
        scratch_shapes=[
            pltpu.VMEM((nblk, N_KV_HEADS, 2 * BLOCK, LANES), BF16),
            pltpu.VMEM((tm + BLOCK, 2 * N_KV_HEADS * LANES), BF16),
            pltpu.VMEM((N_KV_HEADS * 2 * HEAD_DIM, tm + BLOCK), BF16),
            pltpu.VMEM((SUBLANES + tm, D_MODEL), F32),
            pltpu.VMEM((tm, D_MODEL), BF16),
            pltpu.VMEM((tm, D_MODEL), F32),
            pltpu.VMEM((tm, D_FF), BF16),
            pltpu.VMEM((D_MODEL, IN_WIDTH), BF16),
            pltpu.VMEM((D_MODEL, D_MODEL), BF16),
            pltpu.VMEM((D_MODEL, D_FF), BF16),
            pltpu.VMEM((D_MODEL, D_FF), BF16),
            pltpu.VMEM((D_FF, D_MODEL), BF16),
            pltpu.VMEM((CAST_SLOTS, CAST_ROWS_IN, IN_WIDTH), F32),
            pltpu.VMEM((CAST_SLOTS, CAST_ROWS_FF, D_FF), F32),
            pltpu.VMEM((CAST_SLOTS, CAST_ROWS_SQ, D_MODEL), F32),
            pltpu.SemaphoreType.DMA((CAST_SLOTS,)),
        ],
        compiler_params=pltpu.CompilerParams(
            dimension_semantics=("arbitrary",), vmem_limit_bytes=VMEM_LIMIT),
        name="block",
    )(sinks, x, w_in, conv_w, w_out, g1, b1, jnp.asarray(_alibi_bias()), w_gate, w_up, w_down, g2, b2)


def kernel(x, w_in, conv_w, attn_sinks, w_out, ln1_g, ln1_b, w_gate, w_up, w_down, ln2_g, ln2_b):
    bsz, seq, _ = x.shape
    assert seq % TILE == 0 and TILE % BLOCK == 0
    x = x.reshape(bsz * seq, D_MODEL)
    for l in range(DEPTH):
        x = _block_call(x, attn_sinks[l], w_in[l], conv_w[l], w_out[l],
                        ln1_g[l][None, :], ln1_b[l][None, :], w_gate[l], w_up[l], w_down[l],
                        ln2_g[l][None, :], ln2_b[l][None, :], seq)
    return x.reshape(bsz, seq, D_MODEL)
```

```python
import functools

import numpy as np

import jax
import jax.numpy as jnp
from jax import lax
from jax.experimental import pallas as pl
from jax.experimental.pallas import tpu as pltpu

D_MODEL = 1024
HEAD_DIM = 64
N_Q_HEADS = D_MODEL // HEAD_DIM
N_KV_HEADS = N_Q_HEADS // 4
GQA_GROUP = N_Q_HEADS // N_KV_HEADS
KV_WIDTH = N_KV_HEADS * HEAD_DIM
WINDOW = 128
BLOCK = 128
CONV_K = 3
D_FF = 2816
DEPTH = 1
ALPHA = (2.0 * DEPTH) ** 0.25
LN_EPS = 1e-5
NEG_INF = -1e30

OFF_Q = 0
OFF_K = OFF_Q + D_MODEL
OFF_V = OFF_K + KV_WIDTH
OFF_CB = OFF_V + KV_WIDTH
OFF_CC = OFF_CB + D_MODEL
OFF_CH = OFF_CC + D_MODEL
OFF_GC = OFF_CH + D_MODEL
OFF_GA = OFF_GC + D_MODEL
IN_WIDTH = OFF_GA + D_MODEL

LANES = 128
SUBLANES = 8
TILE = 256
CONV_CHUNK = GQA_GROUP * HEAD_DIM
FFN_CHUNK = 256
N_FFN_CHUNKS = D_FF // FFN_CHUNK
FFN_HEAD_CHUNKS = N_FFN_CHUNKS - 2 * N_KV_HEADS
VMEM_LIMIT = 58 * 1024 * 1024
CAST_SLOTS = 3
CAST_ROWS_IN = 64
CAST_ROWS_FF = 128
CAST_ROWS_SQ = 256

F32 = jnp.float32
BF16 = jnp.bfloat16


def _alibi_bias():
    slopes = np.array([2.0 ** (-8.0 * (h + 1) / N_Q_HEADS) for h in range(N_Q_HEADS)], dtype=np.float32)
    kj = np.arange(BLOCK)[:, None]
    qi = np.arange(BLOCK)[None, :]
    dist = ((qi - kj) % BLOCK).astype(np.float32)
    return -slopes[:, None, None] * dist[None]


def _dot(a, b):
    return jnp.dot(a, b, preferred_element_type=F32)


def _dot_nt(a, b):
    return lax.dot_general(a, b, (((1,), (1,)), ((), ())), preferred_element_type=F32)


def _layer_norm(z, g, b):
    mu = jnp.mean(z, axis=-1, keepdims=True)
    d = z - mu
    var = jnp.mean(d * d, axis=-1, keepdims=True)
    return d * lax.rsqrt(var + LN_EPS) * g + b


def _cast_weight(src_hbm, dst_scr, stage, sem):
    slots, rows = stage.shape[0], stage.shape[1]
    n = src_hbm.shape[0] // rows
    assert n * rows == src_hbm.shape[0] and stage.shape[2] == src_hbm.shape[1] and n >= slots

    def chunk_copy(k):
        slot = lax.rem(k, slots)
        return pltpu.make_async_copy(src_hbm.at[pl.ds(k * rows, rows), :], stage.at[slot], sem.at[slot])

    for k in range(slots - 1):
        chunk_copy(k).start()

    def body(k, carry):
        @pl.when(k + slots - 1 < n)
        def _():
            chunk_copy(k + slots - 1).start()

        chunk_copy(k).wait()
        dst_scr[pl.ds(pl.multiple_of(k * rows, rows), rows), :] = stage[lax.rem(k, slots)].astype(BF16)
        return carry

    lax.fori_loop(0, n, body, 0)


def _block_kernel(sinks_ref, x_ref, w_in_hbm, conv_w_ref, w_out_hbm, g1_ref, b1_ref, bias_ref,
                  w_gate_hbm, w_up_hbm, w_down_hbm, g2_ref, b2_ref, o_ref,
                  q_scr, k_scr, vt_scr, ucarry_scr, merged_scr, z1_scr, hid_scr,
                  w_in_ref, w_out_ref, w_gate_ref, w_up_ref, w_down_ref, stage_in, stage_ff, stage_sq, cast_sem,
                  *, tiles_per_seq):
    tm = x_ref.shape[0]
    nblk = tm // BLOCK
    s = pl.program_id(0)
    i = lax.rem(s, tiles_per_seq)
    seq_start = i == 0

    @pl.when(s == 0)
    def _():
        z1_scr[...] = jnp.zeros(z1_scr.shape, F32)
        k_scr[...] = jnp.zeros(k_scr.shape, BF16)
        vt_scr[...] = jnp.zeros(vt_scr.shape, BF16)
        ucarry_scr[...] = jnp.zeros(ucarry_scr.shape, F32)
        _cast_weight(w_in_hbm, w_in_ref, stage_in, cast_sem)
        _cast_weight(w_out_hbm, w_out_ref, stage_sq, cast_sem)
        _cast_weight(w_gate_hbm, w_gate_ref, stage_ff, cast_sem)
        _cast_weight(w_up_hbm, w_up_ref, stage_ff, cast_sem)
        _cast_weight(w_down_hbm, w_down_ref, stage_sq, cast_sem)

    x1 = _layer_norm(z1_scr[...], g1_ref[...], b1_ref[...])
    x1b = x1.astype(BF16)

    def gate_up(c):
        c0 = c * FFN_CHUNK
        gate = _dot(x1b, w_gate_ref[:, c0:c0 + FFN_CHUNK])
        up = _dot(x1b, w_up_ref[:, c0:c0 + FFN_CHUNK])
        hid_scr[:, c0:c0 + FFN_CHUNK] = (gate * jax.nn.sigmoid(gate) * up).astype(BF16)

    k_scr[0:BLOCK, :] = jnp.where(seq_start, jnp.zeros((BLOCK, k_scr.shape[1]), BF16), k_scr[tm:tm + BLOCK, :])
    vt_scr[:, 0:BLOCK] = jnp.where(seq_start, jnp.zeros((vt_scr.shape[0], BLOCK), BF16), vt_scr[:, tm:tm + BLOCK])

    xb = x_ref[...].astype(BF16)
    lane = lax.broadcasted_iota(jnp.int32, (tm, LANES), 1)
    lo = lane < HEAD_DIM

    qkv = _dot(xb, w_in_ref[:, OFF_Q:OFF_CB])
    for blk in range(nblk):
        for pr in range(N_Q_HEADS // 2):
            qp = qkv[blk * BLOCK:(blk + 1) * BLOCK, pr * LANES:(pr + 1) * LANES] * (HEAD_DIM ** -0.5)
            q_scr[blk, pr // 2, (pr % 2) * BLOCK:(pr % 2 + 1) * BLOCK, :] = qp.astype(BF16)

    for t in range(KV_WIDTH // LANES):
        kt = qkv[:, OFF_K + t * LANES:OFF_K + (t + 1) * LANES]
        kr = pltpu.roll(kt, HEAD_DIM, axis=1)
        zero = jnp.zeros_like(kt)
        h0, h1 = 2 * t, 2 * t + 1
        k_scr[BLOCK:, (2 * h0) * LANES:(2 * h0 + 1) * LANES] = jnp.where(lo, kt, zero).astype(BF16)
        k_scr[BLOCK:, (2 * h0 + 1) * LANES:(2 * h0 + 2) * LANES] = jnp.where(lo, zero, kr).astype(BF16)
        k_scr[BLOCK:, (2 * h1) * LANES:(2 * h1 + 1) * LANES] = jnp.where(lo, kr, zero).astype(BF16)
        k_scr[BLOCK:, (2 * h1 + 1) * LANES:(2 * h1 + 2) * LANES] = jnp.where(lo, zero, kt).astype(BF16)
    v_t = qkv[:, OFF_V:OFF_V + KV_WIDTH].T
    for h in range(N_KV_HEADS):
        vh = v_t[h * HEAD_DIM:(h + 1) * HEAD_DIM, :].astype(BF16)
        vt_scr[(2 * h) * HEAD_DIM:(2 * h + 1) * HEAD_DIM, BLOCK:] = vh
        vt_scr[(2 * h + 1) * HEAD_DIM:(2 * h + 2) * HEAD_DIM, BLOCK:] = vh

    for c in range(FFN_HEAD_CHUNKS):
        gate_up(c)

    kj = lax.broadcasted_iota(jnp.int32, (BLOCK, BLOCK), 0)
    qi = lax.broadcasted_iota(jnp.int32, (BLOCK, BLOCK), 1)
    tri = kj <= qi
    top = kj < HEAD_DIM
    valid0 = (kj - qi) <= jnp.where(seq_start, 0, BLOCK)
    row = lax.broadcasted_iota(jnp.int32, (tm, CONV_CHUNK), 0)

    def scores(blk, h):
        r0 = blk * BLOCK
        qg = q_scr[blk, h]
        ke = k_scr[r0:r0 + 2 * BLOCK, (2 * h) * LANES:(2 * h + 1) * LANES]
        ko = k_scr[r0:r0 + 2 * BLOCK, (2 * h + 1) * LANES:(2 * h + 2) * LANES]
        se = _dot_nt(ke, qg)
        so = _dot_nt(ko, qg)
        return se, so

    def attend(blk, h, se, so):
        r0 = blk * BLOCK
        probs, inv = [], []
        for g, (sc, half) in enumerate(((se, 0), (so, 0), (se, 1), (so, 1))):
            head = GQA_GROUP * h + g
            s_prev = sc[0:BLOCK, half * BLOCK:(half + 1) * BLOCK]
            s_cur = sc[BLOCK:2 * BLOCK, half * BLOCK:(half + 1) * BLOCK]
            sm = jnp.where(tri, s_cur, s_prev) + bias_ref[head]
            if blk == 0:
                sm = jnp.where(valid0, sm, NEG_INF)
            sink = sinks_ref[head]
            m = jnp.maximum(jnp.max(sm, axis=0, keepdims=True), sink)
            p = jnp.exp(sm - m)
            denom = jnp.sum(p, axis=0, keepdims=True) + jnp.exp(sink - m)
            inv.append(1.0 / denom)
            zero = jnp.zeros_like(p)
            probs.append(jnp.concatenate([jnp.where(tri, zero, p), jnp.where(tri, p, zero)],
                                         axis=0).astype(BF16))
        p_t = jnp.concatenate(probs, axis=1)
        vt = vt_scr[(2 * h) * HEAD_DIM:(2 * h + 2) * HEAD_DIM, r0:r0 + 2 * BLOCK]
        o_t = _dot(vt, p_t) * jnp.concatenate(inv, axis=1)
        pairs = []
        for pr in range(2):
            oe = o_t[:, (2 * pr) * BLOCK:(2 * pr + 1) * BLOCK]
            oo = o_t[:, (2 * pr + 1) * BLOCK:(2 * pr + 2) * BLOCK]
            pairs.append(jnp.where(top, oe, oo).T)
        return jnp.concatenate(pairs, axis=1)

    def proj(off, h):
        c0 = off + h * CONV_CHUNK
        return _dot(xb, w_in_ref[:, c0:c0 + CONV_CHUNK])

    units = [(h, blk) for h in range(N_KV_HEADS) for blk in range(nblk)]
    assert nblk == 2
    sc = scores(0, 0)
    for n, (h, blk) in enumerate(units):
        c0 = h * CONV_CHUNK
        if blk == 0:
            cb, cc, ch = proj(OFF_CB, h), proj(OFF_CC, h), proj(OFF_CH, h)
            y_blocks = []
        else:
            gc, ga = proj(OFF_GC, h), proj(OFF_GA, h)
        gate_up(FFN_HEAD_CHUNKS + n)
        if n + 1 < len(units):
            sc_next = scores(units[n + 1][1], units[n + 1][0])
        y_blocks.append(attend(blk, h, *sc))
        sc = sc_next
        if blk == 0:
            continue
        y_attn = jnp.concatenate(y_blocks, axis=0)
        u = cc * ch
        zrow = jnp.zeros((1, CONV_CHUNK), F32)
        last = jnp.where(seq_start, zrow, ucarry_scr[SUBLANES - 1:SUBLANES, c0:c0 + CONV_CHUNK])
        prev = jnp.where(seq_start, zrow, ucarry_scr[SUBLANES - 2:SUBLANES - 1, c0:c0 + CONV_CHUNK])
        u1 = jnp.where(row == 0, last, pltpu.roll(u, 1, axis=0))
        u2 = jnp.where(row == 0, prev, jnp.where(row == 1, last, pltpu.roll(u, 2, axis=0)))
        w = conv_w_ref[:, c0:c0 + CONV_CHUNK]
        y = w[0:1, :] * u2 + w[1:2, :] * u1 + w[2:3, :] * u
        ucarry_scr[:, c0:c0 + CONV_CHUNK] = u[tm - SUBLANES:tm, :]
        merged = jax.nn.sigmoid(gc) * (cb * y) + jax.nn.sigmoid(ga) * y_attn
        merged_scr[:, c0:c0 + CONV_CHUNK] = merged.astype(BF16)

    ffn_a = _dot(hid_scr[0:BLOCK, :], w_down_ref[...])
    ya = _layer_norm(ALPHA * x1[0:BLOCK, :] + ffn_a, g2_ref[...], b2_ref[...])
    o_ref[0:BLOCK, :] = ya
    bits = lax.bitcast_convert_type(ya, jnp.uint32)
    folded = bits[:, 0:LANES]
    for c in range(1, D_MODEL // LANES):
        folded = folded | bits[:, c * LANES:(c + 1) * LANES]
    plus_zero = lax.bitcast_convert_type((folded >> 16) >> 16, F32)
    ffn_b = _dot(hid_scr[BLOCK:2 * BLOCK, :], w_down_ref[...])
    zb = ALPHA * x1[BLOCK:2 * BLOCK, :] + ffn_b
    zb = jnp.concatenate([zb[:, 0:LANES] + plus_zero, zb[:, LANES:]], axis=1)
    o_ref[BLOCK:2 * BLOCK, :] = _layer_norm(zb, g2_ref[...], b2_ref[...])

    mix = _dot(merged_scr[...], w_out_ref[...])
    z1_scr[...] = ALPHA * x_ref[...] + mix


def _resident(shape):
    return pl.BlockSpec(shape, lambda *_: (0,) * len(shape), pipeline_mode=pl.Buffered(1))


def _block_call(x, sinks, w_in, conv_w, w_out, g1, b1, w_gate, w_up, w_down, g2, b2, seq):
    tokens = x.shape[0]
    tm = TILE
    nblk = tm // BLOCK
    n_tiles = tokens // tm
    in_tile = pl.BlockSpec((tm, D_MODEL), lambda s: (jnp.minimum(s, n_tiles - 1), 0))
    out_tile = pl.BlockSpec((tm, D_MODEL), lambda s: (jnp.maximum(s - 1, 0), 0))
    in_hbm = pl.BlockSpec(memory_space=pl.ANY)
    return pl.pallas_call(
        functools.partial(_block_kernel, tiles_per_seq=seq // tm),
        grid=(n_tiles + 1,),
        in_specs=[
            pl.BlockSpec(memory_space=pltpu.SMEM),
            in_tile,
            in_hbm,
            _resident((CONV_K, D_MODEL)),
            in_hbm,
            _resident((1, D_MODEL)),
            _resident((1, D_MODEL)),
            _resident((N_Q_HEADS, BLOCK, BLOCK)),
            in_hbm,
            in_hbm,
            in_hbm,
            _resident((1, D_MODEL)),
            _resident((1, D_MODEL)),
        ],
        out_specs=out_tile,
        out_shape=jax.ShapeDtypeStruct(x.shape, F32),
        scratch_shapes=[
            pltpu.VMEM((nblk, N_KV_HEADS, 2 * BLOCK, LANES), BF16),
            pltpu.VMEM((tm + BLOCK, 2 * N_KV_HEADS * LANES), BF16),
            pltpu.VMEM((N_KV_HEADS * 2 * HEAD_DIM, tm + BLOCK), BF16),
            pltpu.VMEM((SUBLANES, D_MODEL), F32),
            pltpu.VMEM((tm, D_MODEL), BF16),
            pltpu.VMEM((tm, D_MODEL), F32),
            pltpu.VMEM((tm, D_FF), BF16),
            pltpu.VMEM((D_MODEL, IN_WIDTH), BF16),
            pltpu.VMEM((D_MODEL, D_MODEL), BF16),
            pltpu.VMEM((D_MODEL, D_FF), BF16),
            pltpu.VMEM((D_MODEL, D_FF), BF16),
            pltpu.VMEM((D_FF, D_MODEL), BF16),
            pltpu.VMEM((CAST_SLOTS, CAST_ROWS_IN, IN_WIDTH), F32),
            pltpu.VMEM((CAST_SLOTS, CAST_ROWS_FF, D_FF), F32),
            pltpu.VMEM((CAST_SLOTS, CAST_ROWS_SQ, D_MODEL), F32),
            pltpu.SemaphoreType.DMA((CAST_SLOTS,)),
        ],
        compiler_params=pltpu.CompilerParams(
            dimension_semantics=("arbitrary",), vmem_limit_bytes=VMEM_LIMIT, skip_device_barrier=True),
        name="block",
    )(sinks, x, w_in, conv_w, w_out, g1, b1, jnp.asarray(_alibi_bias()), w_gate, w_up, w_down, g2, b2)


def kernel(x, w_in, conv_w, attn_sinks, w_out, ln1_g, ln1_b, w_gate, w_up, w_down, ln2_g, ln2_b):
    bsz, seq, _ = x.shape
    assert seq % TILE == 0 and TILE % BLOCK == 0
    x = x.reshape(bsz * seq, D_MODEL)
    for l in range(DEPTH):
        x = _block_call(x, attn_sinks[l], w_in[l], conv_w[l], w_out[l],
                        ln1_g[l][None, :], ln1_b[l][None, :], w_gate[l], w_up[l], w_down[l],
                        ln2_g[l][None, :], ln2_b[l][None, :], seq)
    return x.reshape(bsz, seq, D_MODEL)
```

```python
import functools

import numpy as np

import jax
import jax.numpy as jnp
from jax import lax
from jax.experimental import pallas as pl
from jax.experimental.pallas import tpu as pltpu

D_MODEL = 1024
HEAD_DIM = 64
N_Q_HEADS = D_MODEL // HEAD_DIM
N_KV_HEADS = N_Q_HEADS // 4
GQA_GROUP = N_Q_HEADS // N_KV_HEADS
KV_WIDTH = N_KV_HEADS * HEAD_DIM
WINDOW = 128
BLOCK = 128
CONV_K = 3
D_FF = 2816
DEPTH = 1
ALPHA = (2.0 * DEPTH) ** 0.25
LN_EPS = 1e-5
NEG_INF = -1e30

OFF_Q = 0
OFF_K = OFF_Q + D_MODEL
OFF_V = OFF_K + KV_WIDTH
OFF_CB = OFF_V + KV_WIDTH
OFF_CC = OFF_CB + D_MODEL
OFF_CH = OFF_CC + D_MODEL
OFF_GC = OFF_CH + D_MODEL
OFF_GA = OFF_GC + D_MODEL
IN_WIDTH = OFF_GA + D_MODEL

LANES = 128
SUBLANES = 8
TILE = 256
CONV_CHUNK = GQA_GROUP * HEAD_DIM
FFN_CHUNK = 256
N_FFN_CHUNKS = D_FF // FFN_CHUNK
FFN_HEAD_CHUNKS = N_FFN_CHUNKS - 2 * N_KV_HEADS
VMEM_LIMIT = 58 * 1024 * 1024
CAST_SLOTS = 3
CAST_ROWS_IN = 64
CAST_ROWS_FF = 128
CAST_ROWS_SQ = 256

F32 = jnp.float32
BF16 = jnp.bfloat16


def _alibi_bias():
    slopes = np.array([2.0 ** (-8.0 * (h + 1) / N_Q_HEADS) for h in range(N_Q_HEADS)], dtype=np.float32)
    kj = np.arange(BLOCK)[:, None]
    qi = np.arange(BLOCK)[None, :]
    dist = ((qi - kj) % BLOCK).astype(np.float32)
    return -slopes[:, None, None] * dist[None]


def _dot(a, b):
    return jnp.dot(a, b, preferred_element_type=F32)


def _dot_nt(a, b):
    return lax.dot_general(a, b, (((1,), (1,)), ((), ())), preferred_element_type=F32)


def _layer_norm(z, g, b):
    mu = jnp.mean(z, axis=-1, keepdims=True)
    d = z - mu
    var = jnp.mean(d * d, axis=-1, keepdims=True)
    return d * lax.rsqrt(var + LN_EPS) * g + b


def _cast_weight(src_hbm, dst_scr, stage, sem):
    slots, rows = stage.shape[0], stage.shape[1]
    n = src_hbm.shape[0] // rows
    assert n * rows == src_hbm.shape[0] and stage.shape[2] == src_hbm.shape[1] and n >= slots

    def chunk_copy(k):
        slot = lax.rem(k, slots)
        return pltpu.make_async_copy(src_hbm.at[pl.ds(k * rows, rows), :], stage.at[slot], sem.at[slot])

    for k in range(slots - 1):
        chunk_copy(k).start()

    def body(k, carry):
        @pl.when(k + slots - 1 < n)
        def _():
            chunk_copy(k + slots - 1).start()

        chunk_copy(k).wait()
        dst_scr[pl.ds(pl.multiple_of(k * rows, rows), rows), :] = stage[lax.rem(k, slots)].astype(BF16)
        return carry

    lax.fori_loop(0, n, body, 0)


def _block_kernel(sinks_ref, x_hbm, w_in_hbm, conv_w_ref, w_out_hbm, g1_ref, b1_ref, bias_ref,
                  w_gate_hbm, w_up_hbm, w_down_hbm, g2_ref, b2_ref, o_hbm,
                  q_scr, k_scr, vt_scr, ucarry_scr, merged_scr, z1_scr, hid_scr,
                  w_in_ref, w_out_ref, w_gate_ref, w_up_ref, w_down_ref, stage_in, stage_ff, stage_sq, cast_sem,
                  *, tiles_per_seq, n_tiles, tm):
    z1_scr[...] = jnp.zeros(z1_scr.shape, F32)
    k_scr[...] = jnp.zeros(k_scr.shape, BF16)
    vt_scr[...] = jnp.zeros(vt_scr.shape, BF16)
    ucarry_scr[...] = jnp.zeros(ucarry_scr.shape, F32)
    _cast_weight(w_in_hbm, w_in_ref, stage_in, cast_sem)
    _cast_weight(w_out_hbm, w_out_ref, stage_sq, cast_sem)
    _cast_weight(w_gate_hbm, w_gate_ref, stage_ff, cast_sem)
    _cast_weight(w_up_hbm, w_up_ref, stage_ff, cast_sem)
    _cast_weight(w_down_hbm, w_down_ref, stage_sq, cast_sem)

    def step(x_ref, o_ref):
        _tile_step(sinks_ref, x_ref, conv_w_ref, g1_ref, b1_ref, bias_ref, g2_ref, b2_ref, o_ref,
                   q_scr, k_scr, vt_scr, ucarry_scr, merged_scr, z1_scr, hid_scr,
                   w_in_ref, w_out_ref, w_gate_ref, w_up_ref, w_down_ref, tiles_per_seq=tiles_per_seq)

    in_tile = pl.BlockSpec((tm, D_MODEL), lambda s: (jnp.minimum(s, n_tiles - 1), 0))
    out_tile = pl.BlockSpec((tm, D_MODEL), lambda s: (jnp.maximum(s - 1, 0), 0))
    pltpu.emit_pipeline(step, grid=(n_tiles + 1,), in_specs=[in_tile], out_specs=[out_tile])(x_hbm, o_hbm)


def _tile_step(sinks_ref, x_ref, conv_w_ref, g1_ref, b1_ref, bias_ref, g2_ref, b2_ref, o_ref,
               q_scr, k_scr, vt_scr, ucarry_scr, merged_scr, z1_scr, hid_scr,
               w_in_ref, w_out_ref, w_gate_ref, w_up_ref, w_down_ref, *, tiles_per_seq):
    tm = x_ref.shape[0]
    nblk = tm // BLOCK
    s = pl.program_id(0)
    i = lax.rem(s, tiles_per_seq)
    seq_start = i == 0

    x1 = _layer_norm(z1_scr[...], g1_ref[...], b1_ref[...])
    x1b = x1.astype(BF16)

    def gate_up(c):
        c0 = c * FFN_CHUNK
        gate = _dot(x1b, w_gate_ref[:, c0:c0 + FFN_CHUNK])
        up = _dot(x1b, w_up_ref[:, c0:c0 + FFN_CHUNK])
        hid_scr[:, c0:c0 + FFN_CHUNK] = (gate * jax.nn.sigmoid(gate) * up).astype(BF16)

    k_scr[0:BLOCK, :] = jnp.where(seq_start, jnp.zeros((BLOCK, k_scr.shape[1]), BF16), k_scr[tm:tm + BLOCK, :])
    vt_scr[:, 0:BLOCK] = jnp.where(seq_start, jnp.zeros((vt_scr.shape[0], BLOCK), BF16), vt_scr[:, tm:tm + BLOCK])

    xb = x_ref[...].astype(BF16)
    lane = lax.broadcasted_iota(jnp.int32, (tm, LANES), 1)
    lo = lane < HEAD_DIM

    qkv = _dot(xb, w_in_ref[:, OFF_Q:OFF_CB])
    for blk in range(nblk):
        for pr in range(N_Q_HEADS // 2):
            qp = qkv[blk * BLOCK:(blk + 1) * BLOCK, pr * LANES:(pr + 1) * LANES] * (HEAD_DIM ** -0.5)
            q_scr[blk, pr // 2, (pr % 2) * BLOCK:(pr % 2 + 1) * BLOCK, :] = qp.astype(BF16)

    for t in range(KV_WIDTH // LANES):
        kt = qkv[:, OFF_K + t * LANES:OFF_K + (t + 1) * LANES]
        kr = pltpu.roll(kt, HEAD_DIM, axis=1)
        zero = jnp.zeros_like(kt)
        h0, h1 = 2 * t, 2 * t + 1
        k_scr[BLOCK:, (2 * h0) * LANES:(2 * h0 + 1) * LANES] = jnp.where(lo, kt, zero).astype(BF16)
        k_scr[BLOCK:, (2 * h0 + 1) * LANES:(2 * h0 + 2) * LANES] = jnp.where(lo, zero, kr).astype(BF16)
        k_scr[BLOCK:, (2 * h1) * LANES:(2 * h1 + 1) * LANES] = jnp.where(lo, kr, zero).astype(BF16)
        k_scr[BLOCK:, (2 * h1 + 1) * LANES:(2 * h1 + 2) * LANES] = jnp.where(lo, zero, kt).astype(BF16)
    v_t = qkv[:, OFF_V:OFF_V + KV_WIDTH].T
    for h in range(N_KV_HEADS):
        vh = v_t[h * HEAD_DIM:(h + 1) * HEAD_DIM, :].astype(BF16)
        vt_scr[(2 * h) * HEAD_DIM:(2 * h + 1) * HEAD_DIM, BLOCK:] = vh
        vt_scr[(2 * h + 1) * HEAD_DIM:(2 * h + 2) * HEAD_DIM, BLOCK:] = vh

    for c in range(FFN_HEAD_CHUNKS):
        gate_up(c)

    kj = lax.broadcasted_iota(jnp.int32, (BLOCK, BLOCK), 0)
    qi = lax.broadcasted_iota(jnp.int32, (BLOCK, BLOCK), 1)
    tri = kj <= qi
    top = kj < HEAD_DIM
    valid0 = (kj - qi) <= jnp.where(seq_start, 0, BLOCK)
    row = lax.broadcasted_iota(jnp.int32, (tm, CONV_CHUNK), 0)

    def scores(blk, h):
        r0 = blk * BLOCK
        qg = q_scr[blk, h]
        ke = k_scr[r0:r0 + 2 * BLOCK, (2 * h) * LANES:(2 * h + 1) * LANES]
        ko = k_scr[r0:r0 + 2 * BLOCK, (2 * h + 1) * LANES:(2 * h + 2) * LANES]
        se = _dot_nt(ke, qg)
        so = _dot_nt(ko, qg)
        return se, so

    def attend(blk, h, se, so):
        r0 = blk * BLOCK
        probs, inv = [], []
        for g, (sc, half) in enumerate(((se, 0), (so, 0), (se, 1), (so, 1))):
            head = GQA_GROUP * h + g
            s_prev = sc[0:BLOCK, half * BLOCK:(half + 1) * BLOCK]
            s_cur = sc[BLOCK:2 * BLOCK, half * BLOCK:(half + 1) * BLOCK]
            sm = jnp.where(tri, s_cur, s_prev) + bias_ref[head]
            if blk == 0:
                sm = jnp.where(valid0, sm, NEG_INF)
            sink = sinks_ref[head]
            m = jnp.maximum(jnp.max(sm, axis=0, keepdims=True), sink)
            p = jnp.exp(sm - m)
            denom = jnp.sum(p, axis=0, keepdims=True) + jnp.exp(sink - m)
            inv.append(1.0 / denom)
            zero = jnp.zeros_like(p)
            probs.append(jnp.concatenate([jnp.where(tri, zero, p), jnp.where(tri, p, zero)],
                                         axis=0).astype(BF16))
        p_t = jnp.concatenate(probs, axis=1)
        vt = vt_scr[(2 * h) * HEAD_DIM:(2 * h + 2) * HEAD_DIM, r0:r0 + 2 * BLOCK]
        o_t = _dot(vt, p_t) * jnp.concatenate(inv, axis=1)
        pairs = []
        for pr in range(2):
            oe = o_t[:, (2 * pr) * BLOCK:(2 * pr + 1) * BLOCK]
            oo = o_t[:, (2 * pr + 1) * BLOCK:(2 * pr + 2) * BLOCK]
            pairs.append(jnp.where(top, oe, oo).T)
        return jnp.concatenate(pairs, axis=1)

    def proj(off, h):
        c0 = off + h * CONV_CHUNK
        return _dot(xb, w_in_ref[:, c0:c0 + CONV_CHUNK])

    units = [(h, blk) for h in range(N_KV_HEADS) for blk in range(nblk)]
    assert nblk == 2
    sc = scores(0, 0)
    for n, (h, blk) in enumerate(units):
        c0 = h * CONV_CHUNK
        if blk == 0:
            cb, cc, ch = proj(OFF_CB, h), proj(OFF_CC, h), proj(OFF_CH, h)
            y_blocks = []
        else:
            gc, ga = proj(OFF_GC, h), proj(OFF_GA, h)
        gate_up(FFN_HEAD_CHUNKS + n)
        if n + 1 < len(units):
            sc_next = scores(units[n + 1][1], units[n + 1][0])
        y_blocks.append(attend(blk, h, *sc))
        sc = sc_next
        if blk == 0:
            continue
        y_attn = jnp.concatenate(y_blocks, axis=0)
        u = cc * ch
        zrow = jnp.zeros((1, CONV_CHUNK), F32)
        last = jnp.where(seq_start, zrow, ucarry_scr[SUBLANES - 1:SUBLANES, c0:c0 + CONV_CHUNK])
        prev = jnp.where(seq_start, zrow, ucarry_scr[SUBLANES - 2:SUBLANES - 1, c0:c0 + CONV_CHUNK])
        u1 = jnp.where(row == 0, last, pltpu.roll(u, 1, axis=0))
        u2 = jnp.where(row == 0, prev, jnp.where(row == 1, last, pltpu.roll(u, 2, axis=0)))
        w = conv_w_ref[:, c0:c0 + CONV_CHUNK]
        y = w[0:1, :] * u2 + w[1:2, :] * u1 + w[2:3, :] * u
        ucarry_scr[:, c0:c0 + CONV_CHUNK] = u[tm - SUBLANES:tm, :]
        merged = jax.nn.sigmoid(gc) * (cb * y) + jax.nn.sigmoid(ga) * y_attn
        merged_scr[:, c0:c0 + CONV_CHUNK] = merged.astype(BF16)

    ffn_a = _dot(hid_scr[0:BLOCK, :], w_down_ref[...])
    ya = _layer_norm(ALPHA * x1[0:BLOCK, :] + ffn_a, g2_ref[...], b2_ref[...])
    o_ref[0:BLOCK, :] = ya
    bits = lax.bitcast_convert_type(ya, jnp.uint32)
    folded = bits[:, 0:LANES]
    for c in range(1, D_MODEL // LANES):
        folded = folded | bits[:, c * LANES:(c + 1) * LANES]
    plus_zero = lax.bitcast_convert_type((folded >> 16) >> 16, F32)
    ffn_b = _dot(hid_scr[BLOCK:2 * BLOCK, :], w_down_ref[...])
    zb = ALPHA * x1[BLOCK:2 * BLOCK, :] + ffn_b
    zb = jnp.concatenate([zb[:, 0:LANES] + plus_zero, zb[:, LANES:]], axis=1)
    o_ref[BLOCK:2 * BLOCK, :] = _layer_norm(zb, g2_ref[...], b2_ref[...])

    mix = _dot(merged_scr[...], w_out_ref[...])
    z1_scr[...] = ALPHA * x_ref[...] + mix


def _block_call(x, sinks, w_in, conv_w, w_out, g1, b1, w_gate, w_up, w_down, g2, b2, seq):
    tokens = x.shape[0]
    tm = TILE
    nblk = tm // BLOCK
    n_tiles = tokens // tm
    in_hbm = pl.BlockSpec(memory_space=pl.ANY)
    in_vmem = pl.BlockSpec(memory_space=pltpu.VMEM)
    return pl.pallas_call(
        functools.partial(_block_kernel, tiles_per_seq=seq // tm, n_tiles=n_tiles, tm=tm),
        in_specs=[
            pl.BlockSpec(memory_space=pltpu.SMEM),
            in_hbm,
            in_hbm,
            in_vmem,
            in_hbm,
            in_vmem,
            in_vmem,
            in_vmem,
            in_hbm,
            in_hbm,
            in_hbm,
            in_vmem,
            in_vmem,
        ],
        out_specs=in_hbm,
        out_shape=jax.ShapeDtypeStruct(x.shape, F32),
        scratch_shapes=[
            pltpu.VMEM((nblk, N_KV_HEADS, 2 * BLOCK, LANES), BF16),
            pltpu.VMEM((tm + BLOCK, 2 * N_KV_HEADS * LANES), BF16),
            pltpu.VMEM((N_KV_HEADS * 2 * HEAD_DIM, tm + BLOCK), BF16),
            pltpu.VMEM((SUBLANES, D_MODEL), F32),
            pltpu.VMEM((tm, D_MODEL), BF16),
            pltpu.VMEM((tm, D_MODEL), F32),
            pltpu.VMEM((tm, D_FF), BF16),
            pltpu.VMEM((D_MODEL, IN_WIDTH), BF16),
            pltpu.VMEM((D_MODEL, D_MODEL), BF16),
            pltpu.VMEM((D_MODEL, D_FF), BF16),
            pltpu.VMEM((D_MODEL, D_FF), BF16),
            pltpu.VMEM((D_FF, D_MODEL), BF16),
            pltpu.VMEM((CAST_SLOTS, CAST_ROWS_IN, IN_WIDTH), F32),
            pltpu.VMEM((CAST_SLOTS, CAST_ROWS_FF, D_FF), F32),
            pltpu.VMEM((CAST_SLOTS, CAST_ROWS_SQ, D_MODEL), F32),
            pltpu.SemaphoreType.DMA((CAST_SLOTS,)),
        ],
        compiler_params=pltpu.CompilerParams(vmem_limit_bytes=VMEM_LIMIT),
        name="block",
    )(sinks, x, w_in, conv_w, w_out, g1, b1, jnp.asarray(_alibi_bias()), w_gate, w_up, w_down, g2, b2)


def kernel(x, w_in, conv_w, attn_sinks, w_out, ln1_g, ln1_b, w_gate, w_up, w_down, ln2_g, ln2_b):
    bsz, seq, _ = x.shape
    assert seq % TILE == 0 and TILE % BLOCK == 0
    x = x.reshape(bsz * seq, D_MODEL)
    for l in range(DEPTH):
        x = _block_call(x, attn_sinks[l], w_in[l], conv_w[l], w_out[l],
                        ln1_g[l][None, :], ln1_b[l][None, :], w_gate[l], w_up[l], w_down[l],
                        ln2_g[l][None, :], ln2_b[l][None, :], seq)
    return x.reshape(bsz, seq, D_MODEL)
```

```python
import functools

import numpy as np

import jax
import jax.numpy as jnp
from jax import lax
from jax.experimental import pallas as pl
from jax.experimental.pallas import tpu as pltpu

D_MODEL = 1024
HEAD_DIM = 64
N_Q_HEADS = D_MODEL // HEAD_DIM
N_KV_HEADS = N_Q_HEADS // 4
GQA_GROUP = N_Q_HEADS // N_KV_HEADS
KV_WIDTH = N_KV_HEADS * HEAD_DIM
WINDOW = 128
BLOCK = 128
CONV_K = 3
D_FF = 2816
DEPTH = 1
ALPHA = (2.0 * DEPTH) ** 0.25
LN_EPS = 1e-5
NEG_INF = -1e30

OFF_Q = 0
OFF_K = OFF_Q + D_MODEL
OFF_V = OFF_K + KV_WIDTH
OFF_CB = OFF_V + KV_WIDTH
OFF_CC = OFF_CB + D_MODEL
OFF_CH = OFF_CC + D_MODEL
OFF_GC = OFF_CH + D_MODEL
OFF_GA = OFF_GC + D_MODEL
IN_WIDTH = OFF_GA + D_MODEL

LANES = 128
SUBLANES = 8
TILE = 256
CONV_CHUNK = GQA_GROUP * HEAD_DIM
FFN_CHUNK = 256
N_FFN_CHUNKS = D_FF // FFN_CHUNK
GATE_UP_PLAN = [[0, 1]] + [[2, 3]] + [[4 + n] for n in range(7)]
VMEM_LIMIT = 60 * 1024 * 1024
CAST_SLOTS = 3
CAST_ROWS_IN = 64
CAST_ROWS_FF = 128
CAST_ROWS_SQ = 256

F32 = jnp.float32
BF16 = jnp.bfloat16


def _alibi_bias():
    slopes = np.array([2.0 ** (-8.0 * (h + 1) / N_Q_HEADS) for h in range(N_Q_HEADS)], dtype=np.float32)
    kj = np.arange(BLOCK)[:, None]
    qi = np.arange(BLOCK)[None, :]
    dist = ((qi - kj) % BLOCK).astype(np.float32)
    return -slopes[:, None, None] * dist[None]


def _dot(a, b):
    return jnp.dot(a, b, preferred_element_type=F32)


def _dot_nt(a, b):
    return lax.dot_general(a, b, (((1,), (1,)), ((), ())), preferred_element_type=F32)


def _layer_norm(z, g, b):
    mu = jnp.mean(z, axis=-1, keepdims=True)
    d = z - mu
    var = jnp.mean(d * d, axis=-1, keepdims=True)
    return d * lax.rsqrt(var + LN_EPS) * g + b


def _cast_weight(src_hbm, dst_scr, stage, sem):
    slots, rows = stage.shape[0], stage.shape[1]
    n = src_hbm.shape[0] // rows
    assert n * rows == src_hbm.shape[0] and stage.shape[2] == src_hbm.shape[1] and n >= slots

    def chunk_copy(k):
        slot = lax.rem(k, slots)
        return pltpu.make_async_copy(src_hbm.at[pl.ds(k * rows, rows), :], stage.at[slot], sem.at[slot])

    for k in range(slots - 1):
        chunk_copy(k).start()

    def body(k, carry):
        @pl.when(k + slots - 1 < n)
        def _():
            chunk_copy(k + slots - 1).start()

        chunk_copy(k).wait()
        dst_scr[pl.ds(pl.multiple_of(k * rows, rows), rows), :] = stage[lax.rem(k, slots)].astype(BF16)
        return carry

    lax.fori_loop(0, n, body, 0)


def _block_kernel(sinks_ref, x_hbm, w_in_hbm, conv_w_ref, w_out_hbm, g1_ref, b1_ref, bias_ref,
                  w_gate_hbm, w_up_hbm, w_down_hbm, g2_ref, b2_ref, o_hbm,
                  q_scr, k_scr, vt_scr, ucarry_scr, merged_scr, z1_scr, hid_scr,
                  w_in_ref, w_out_ref, w_gate_ref, w_up_ref, w_down_ref, stage_in, stage_ff, stage_sq, cast_sem,
                  *, tiles_per_seq, n_tiles, tm):
    k_scr[...] = jnp.zeros(k_scr.shape, BF16)
    vt_scr[...] = jnp.zeros(vt_scr.shape, BF16)
    ucarry_scr[...] = jnp.zeros(ucarry_scr.shape, F32)
    _cast_weight(w_in_hbm, w_in_ref, stage_in, cast_sem)
    _cast_weight(w_out_hbm, w_out_ref, stage_sq, cast_sem)
    _cast_weight(w_gate_hbm, w_gate_ref, stage_ff, cast_sem)
    _cast_weight(w_up_hbm, w_up_ref, stage_ff, cast_sem)
    _cast_weight(w_down_hbm, w_down_ref, stage_sq, cast_sem)

    def step(x_ref, o_ref, tile, ffn=True):
        _tile_step(sinks_ref, x_ref, conv_w_ref, g1_ref, b1_ref, bias_ref, g2_ref, b2_ref, o_ref,
                   q_scr, k_scr, vt_scr, ucarry_scr, merged_scr, z1_scr, hid_scr,
                   w_in_ref, w_out_ref, w_gate_ref, w_up_ref, w_down_ref,
                   tile=tile, tiles_per_seq=tiles_per_seq, ffn=ffn)

    edge_tile = stage_sq.at[0]
    assert edge_tile.shape == (tm, D_MODEL)
    pltpu.sync_copy(x_hbm.at[pl.ds(0, tm), :], edge_tile)
    step(edge_tile, None, jnp.int32(0), ffn=False)

    in_tile = pl.BlockSpec((tm, D_MODEL), lambda s: (s + 1, 0))
    out_tile = pl.BlockSpec((tm, D_MODEL), lambda s: (s, 0))
    pltpu.emit_pipeline(lambda x_ref, o_ref: step(x_ref, o_ref, pl.program_id(0) + 1),
                        grid=(n_tiles - 1,), in_specs=[in_tile], out_specs=[out_tile])(x_hbm, o_hbm)

    _ffn_only(z1_scr, g1_ref, b1_ref, w_gate_ref, w_up_ref, w_down_ref, g2_ref, b2_ref, hid_scr, edge_tile)
    pltpu.sync_copy(edge_tile, o_hbm.at[pl.ds((n_tiles - 1) * tm, tm), :])


def _ffn_hidden_chunk(x1b, c, w_gate_ref, w_up_ref, hid_scr):
    c0 = c * FFN_CHUNK
    gate = _dot(x1b, w_gate_ref[:, c0:c0 + FFN_CHUNK])
    up = _dot(x1b, w_up_ref[:, c0:c0 + FFN_CHUNK])
    hid_scr[:, c0:c0 + FFN_CHUNK] = (gate * jax.nn.sigmoid(gate) * up).astype(BF16)


def _ffn_down_norm(x1, hid_scr, w_down_ref, g2_ref, b2_ref, o_ref):
    ffn_a = _dot(hid_scr[0:BLOCK, :], w_down_ref[...])
    ya = _layer_norm(ALPHA * x1[0:BLOCK, :] + ffn_a, g2_ref[...], b2_ref[...])
    o_ref[0:BLOCK, :] = ya
    bits = lax.bitcast_convert_type(ya, jnp.uint32)
    folded = bits[:, 0:LANES]
    for c in range(1, D_MODEL // LANES):
        folded = folded | bits[:, c * LANES:(c + 1) * LANES]
    plus_zero = lax.bitcast_convert_type((folded >> 16) >> 16, F32)
    ffn_b = _dot(hid_scr[BLOCK:2 * BLOCK, :], w_down_ref[...])
    zb = ALPHA * x1[BLOCK:2 * BLOCK, :] + ffn_b
    zb = jnp.concatenate([zb[:, 0:LANES] + plus_zero, zb[:, LANES:]], axis=1)
    o_ref[BLOCK:2 * BLOCK, :] = _layer_norm(zb, g2_ref[...], b2_ref[...])


def _ffn_only(z1_scr, g1_ref, b1_ref, w_gate_ref, w_up_ref, w_down_ref, g2_ref, b2_ref, hid_scr, o_ref):
    x1 = _layer_norm(z1_scr[...], g1_ref[...], b1_ref[...])
    x1b = x1.astype(BF16)
    for c in range(N_FFN_CHUNKS):
        _ffn_hidden_chunk(x1b, c, w_gate_ref, w_up_ref, hid_scr)
    _ffn_down_norm(x1, hid_scr, w_down_ref, g2_ref, b2_ref, o_ref)


def _tile_step(sinks_ref, x_ref, conv_w_ref, g1_ref, b1_ref, bias_ref, g2_ref, b2_ref, o_ref,
               q_scr, k_scr, vt_scr, ucarry_scr, merged_scr, z1_scr, hid_scr,
               w_in_ref, w_out_ref, w_gate_ref, w_up_ref, w_down_ref, *, tile, tiles_per_seq, ffn=True):
    tm = x_ref.shape[0]
    nblk = tm // BLOCK
    seq_start = lax.rem(tile, tiles_per_seq) == 0

    if ffn:
        x1 = _layer_norm(z1_scr[...], g1_ref[...], b1_ref[...])
        x1b = x1.astype(BF16)

    def gate_up(c):
        if ffn:
            _ffn_hidden_chunk(x1b, c, w_gate_ref, w_up_ref, hid_scr)

    k_scr[0:BLOCK, :] = jnp.where(seq_start, jnp.zeros((BLOCK, k_scr.shape[1]), BF16), k_scr[tm:tm + BLOCK, :])
    vt_scr[:, 0:BLOCK] = jnp.where(seq_start, jnp.zeros((vt_scr.shape[0], BLOCK), BF16), vt_scr[:, tm:tm + BLOCK])

    xb = x_ref[...].astype(BF16)
    lane = lax.broadcasted_iota(jnp.int32, (tm, LANES), 1)
    lo = lane < HEAD_DIM

    qkv = _dot(xb, w_in_ref[:, OFF_Q:OFF_CB])
    for blk in range(nblk):
        for pr in range(N_Q_HEADS // 2):
            qp = qkv[blk * BLOCK:(blk + 1) * BLOCK, pr * LANES:(pr + 1) * LANES] * (HEAD_DIM ** -0.5)
            q_scr[blk, pr // 2, (pr % 2) * BLOCK:(pr % 2 + 1) * BLOCK, :] = qp.astype(BF16)

    for t in range(KV_WIDTH // LANES):
        kt = qkv[:, OFF_K + t * LANES:OFF_K + (t + 1) * LANES]
        kr = pltpu.roll(kt, HEAD_DIM, axis=1)
        zero = jnp.zeros_like(kt)
        h0, h1 = 2 * t, 2 * t + 1
        k_scr[BLOCK:, (2 * h0) * LANES:(2 * h0 + 1) * LANES] = jnp.where(lo, kt, zero).astype(BF16)
        k_scr[BLOCK:, (2 * h0 + 1) * LANES:(2 * h0 + 2) * LANES] = jnp.where(lo, zero, kr).astype(BF16)
        k_scr[BLOCK:, (2 * h1) * LANES:(2 * h1 + 1) * LANES] = jnp.where(lo, kr, zero).astype(BF16)
        k_scr[BLOCK:, (2 * h1 + 1) * LANES:(2 * h1 + 2) * LANES] = jnp.where(lo, zero, kt).astype(BF16)
    v_t = qkv[:, OFF_V:OFF_V + KV_WIDTH].T
    for h in range(N_KV_HEADS):
        vh = v_t[h * HEAD_DIM:(h + 1) * HEAD_DIM, :].astype(BF16)
        vt_scr[(2 * h) * HEAD_DIM:(2 * h + 1) * HEAD_DIM, BLOCK:] = vh
        vt_scr[(2 * h + 1) * HEAD_DIM:(2 * h + 2) * HEAD_DIM, BLOCK:] = vh

    for c in GATE_UP_PLAN[0]:
        gate_up(c)

    kj = lax.broadcasted_iota(jnp.int32, (BLOCK, BLOCK), 0)
    qi = lax.broadcasted_iota(jnp.int32, (BLOCK, BLOCK), 1)
    tri = kj <= qi
    top = kj < HEAD_DIM
    valid0 = (kj - qi) <= jnp.where(seq_start, 0, BLOCK)
    row = lax.broadcasted_iota(jnp.int32, (tm, CONV_CHUNK), 0)

    def scores(blk, h):
        r0 = blk * BLOCK
        qg = q_scr[blk, h]
        ke = k_scr[r0:r0 + 2 * BLOCK, (2 * h) * LANES:(2 * h + 1) * LANES]
        ko = k_scr[r0:r0 + 2 * BLOCK, (2 * h + 1) * LANES:(2 * h + 2) * LANES]
        se = _dot_nt(ke, qg)
        so = _dot_nt(ko, qg)
        return se, so

    def attend(blk, h, se, so):
        r0 = blk * BLOCK
        probs, inv = [], []
        for g, (sc, half) in enumerate(((se, 0), (so, 0), (se, 1), (so, 1))):
            head = GQA_GROUP * h + g
            s_prev = sc[0:BLOCK, half * BLOCK:(half + 1) * BLOCK]
            s_cur = sc[BLOCK:2 * BLOCK, half * BLOCK:(half + 1) * BLOCK]
            sm = jnp.where(tri, s_cur, s_prev) + bias_ref[head]
            if blk == 0:
                sm = jnp.where(valid0, sm, NEG_INF)
            sink = sinks_ref[head]
            m = jnp.maximum(jnp.max(sm, axis=0, keepdims=True), sink)
            p = jnp.exp(sm - m)
            denom = jnp.sum(p, axis=0, keepdims=True) + jnp.exp(sink - m)
            inv.append(1.0 / denom)
            zero = jnp.zeros_like(p)
            probs.append(jnp.concatenate([jnp.where(tri, zero, p), jnp.where(tri, p, zero)],
                                         axis=0).astype(BF16))
        p_t = jnp.concatenate(probs, axis=1)
        vt = vt_scr[(2 * h) * HEAD_DIM:(2 * h + 2) * HEAD_DIM, r0:r0 + 2 * BLOCK]
        o_t = _dot(vt, p_t) * jnp.concatenate(inv, axis=1)
        pairs = []
        for pr in range(2):
            oe = o_t[:, (2 * pr) * BLOCK:(2 * pr + 1) * BLOCK]
            oo = o_t[:, (2 * pr + 1) * BLOCK:(2 * pr + 2) * BLOCK]
            pairs.append(jnp.where(top, oe, oo).T)
        return jnp.concatenate(pairs, axis=1)

    def proj(off, h):
        c0 = off + h * CONV_CHUNK
        return _dot(xb, w_in_ref[:, c0:c0 + CONV_CHUNK])

    units = [(h, blk) for h in range(N_KV_HEADS) for blk in range(nblk)]
    assert nblk == 2
    sc = scores(0, 0)
    for n, (h, blk) in enumerate(units):
        c0 = h * CONV_CHUNK
        if blk == 0:
            cb, cc, ch = proj(OFF_CB, h), proj(OFF_CC, h), proj(OFF_CH, h)
            y_blocks = []
        else:
            gc, ga = proj(OFF_GC, h), proj(OFF_GA, h)
        for c in GATE_UP_PLAN[1 + n]:
            gate_up(c)
        if n + 1 < len(units):
            sc_next = scores(units[n + 1][1], units[n + 1][0])
        y_blocks.append(attend(blk, h, *sc))
        sc = sc_next
        if blk == 0:
            continue
        y_attn = jnp.concatenate(y_blocks, axis=0)
        u = cc * ch
        zrow = jnp.zeros((1, CONV_CHUNK), F32)
        last = jnp.where(seq_start, zrow, ucarry_scr[SUBLANES - 1:SUBLANES, c0:c0 + CONV_CHUNK])
        prev = jnp.where(seq_start, zrow, ucarry_scr[SUBLANES - 2:SUBLANES - 1, c0:c0 + CONV_CHUNK])
        u1 = jnp.where(row == 0, last, pltpu.roll(u, 1, axis=0))
        u2 = jnp.where(row == 0, prev, jnp.where(row == 1, last, pltpu.roll(u, 2, axis=0)))
        w = conv_w_ref[:, c0:c0 + CONV_CHUNK]
        y = w[0:1, :] * u2 + w[1:2, :] * u1 + w[2:3, :] * u
        ucarry_scr[:, c0:c0 + CONV_CHUNK] = u[tm - SUBLANES:tm, :]
        merged = jax.nn.sigmoid(gc) * (cb * y) + jax.nn.sigmoid(ga) * y_attn
        merged_scr[:, c0:c0 + CONV_CHUNK] = merged.astype(BF16)

    if ffn:
        _ffn_down_norm(x1, hid_scr, w_down_ref, g2_ref, b2_ref, o_ref)

    mix = _dot(merged_scr[...], w_out_ref[...])
    z1_scr[...] = ALPHA * x_ref[...] + mix


def _block_call(x, sinks, w_in, conv_w, w_out, g1, b1, w_gate, w_up, w_down, g2, b2, seq):
    tokens = x.shape[0]
    tm = TILE
    nblk = tm // BLOCK
    n_tiles = tokens // tm
    in_hbm = pl.BlockSpec(memory_space=pl.ANY)
    in_vmem = pl.BlockSpec(memory_space=pltpu.VMEM)
    return pl.pallas_call(
        functools.partial(_block_kernel, tiles_per_seq=seq // tm, n_tiles=n_tiles, tm=tm),
        in_specs=[
            pl.BlockSpec(memory_space=pltpu.SMEM),
            in_hbm,
            in_hbm,
            in_vmem,
            in_hbm,
            in_vmem,
            in_vmem,
            in_vmem,
            in_hbm,
            in_hbm,
            in_hbm,
            in_vmem,
            in_vmem,
        ],
        out_specs=in_hbm,
        out_shape=jax.ShapeDtypeStruct(x.shape, F32),
        scratch_shapes=[
            pltpu.VMEM((nblk, N_KV_HEADS, 2 * BLOCK, LANES), BF16),
            pltpu.VMEM((tm + BLOCK, 2 * N_KV_HEADS * LANES), BF16),
            pltpu.VMEM((N_KV_HEADS * 2 * HEAD_DIM, tm + BLOCK), BF16),
            pltpu.VMEM((SUBLANES, D_MODEL), F32),
            pltpu.VMEM((tm, D_MODEL), BF16),
            pltpu.VMEM((tm, D_MODEL), F32),
            pltpu.VMEM((tm, D_FF), BF16),
            pltpu.VMEM((D_MODEL, IN_WIDTH), BF16),
            pltpu.VMEM((D_MODEL, D_MODEL), BF16),
            pltpu.VMEM((D_MODEL, D_FF), BF16),
            pltpu.VMEM((D_MODEL, D_FF), BF16),
            pltpu.VMEM((D_FF, D_MODEL), BF16),
            pltpu.VMEM((CAST_SLOTS, CAST_ROWS_IN, IN_WIDTH), F32),
            pltpu.VMEM((CAST_SLOTS, CAST_ROWS_FF, D_FF), F32),
            pltpu.VMEM((CAST_SLOTS, CAST_ROWS_SQ, D_MODEL), F32),
            pltpu.SemaphoreType.DMA((CAST_SLOTS,)),
        ],
        compiler_params=pltpu.CompilerParams(vmem_limit_bytes=VMEM_LIMIT),
        name="block",
    )(sinks, x, w_in, conv_w, w_out, g1, b1, jnp.asarray(_alibi_bias()), w_gate, w_up, w_down, g2, b2)


def kernel(x, w_in, conv_w, attn_sinks, w_out, ln1_g, ln1_b, w_gate, w_up, w_down, ln2_g, ln2_b):
    bsz, seq, _ = x.shape
    assert seq % TILE == 0 and TILE % BLOCK == 0
    x = x.reshape(bsz * seq, D_MODEL)
    for l in range(DEPTH):
        x = _block_call(x, attn_sinks[l], w_in[l], conv_w[l], w_out[l],
                        ln1_g[l][None, :], ln1_b[l][None, :], w_gate[l], w_up[l], w_down[l],
                        ln2_g[l][None, :], ln2_b[l][None, :], seq)
    return x.reshape(bsz, seq, D_MODEL)
```

```python
import functools

import numpy as np

import jax
import jax.numpy as jnp
from jax import lax
from jax.experimental import pallas as pl
from jax.experimental.pallas import tpu as pltpu

D_MODEL = 1024
HEAD_DIM = 64
N_Q_HEADS = D_MODEL // HEAD_DIM
N_KV_HEADS = N_Q_HEADS // 4
GQA_GROUP = N_Q_HEADS // N_KV_HEADS
KV_WIDTH = N_KV_HEADS * HEAD_DIM
WINDOW = 128
BLOCK = 128
CONV_K = 3
D_FF = 2816
DEPTH = 1
ALPHA = (2.0 * DEPTH) ** 0.25
LN_EPS = 1e-5
NEG_INF = -1e30

OFF_Q = 0
OFF_K = OFF_Q + D_MODEL
OFF_V = OFF_K + KV_WIDTH
OFF_CB = OFF_V + KV_WIDTH
OFF_CC = OFF_CB + D_MODEL
OFF_CH = OFF_CC + D_MODEL
OFF_GC = OFF_CH + D_MODEL
OFF_GA = OFF_GC + D_MODEL
IN_WIDTH = OFF_GA + D_MODEL

LANES = 128
SUBLANES = 8
TILE = 256
CONV_CHUNK = GQA_GROUP * HEAD_DIM
FFN_CHUNK = 256
N_FFN_CHUNKS = D_FF // FFN_CHUNK
FFN_HEAD_CHUNKS = N_FFN_CHUNKS - 2 * N_KV_HEADS
VMEM_LIMIT = 58 * 1024 * 1024
CAST_SLOTS = 3
CAST_ROWS_IN = 64
CAST_ROWS_FF = 128
CAST_ROWS_SQ = 256

F32 = jnp.float32
BF16 = jnp.bfloat16


def _alibi_bias():
    slopes = np.array([2.0 ** (-8.0 * (h + 1) / N_Q_HEADS) for h in range(N_Q_HEADS)], dtype=np.float32)
    kj = np.arange(BLOCK)[:, None]
    qi = np.arange(BLOCK)[None, :]
    dist = ((qi - kj) % BLOCK).astype(np.float32)
    return -slopes[:, None, None] * dist[None]


def _dot(a, b):
    return jnp.dot(a, b, preferred_element_type=F32)


def _dot_nt(a, b):
    return lax.dot_general(a, b, (((1,), (1,)), ((), ())), preferred_element_type=F32)


def _layer_norm(z, g, b):
    mu = jnp.mean(z, axis=-1, keepdims=True)
    d = z - mu
    var = jnp.mean(d * d, axis=-1, keepdims=True)
    return d * lax.rsqrt(var + LN_EPS) * g + b


def _cast_weight(src_hbm, dst_scr, stage, sem):
    slots, rows = stage.shape[0], stage.shape[1]
    n = src_hbm.shape[0] // rows
    assert n * rows == src_hbm.shape[0] and stage.shape[2] == src_hbm.shape[1] and n >= slots

    def chunk_copy(k):
        slot = lax.rem(k, slots)
        return pltpu.make_async_copy(src_hbm.at[pl.ds(k * rows, rows), :], stage.at[slot], sem.at[slot])

    for k in range(slots - 1):
        chunk_copy(k).start()

    def body(k, carry):
        @pl.when(k + slots - 1 < n)
        def _():
            chunk_copy(k + slots - 1).start()

        chunk_copy(k).wait()
        dst_scr[pl.ds(pl.multiple_of(k * rows, rows), rows), :] = stage[lax.rem(k, slots)].astype(BF16)
        return carry

    lax.fori_loop(0, n, body, 0)


def _block_kernel(sinks_ref, x_hbm, w_in_hbm, conv_w_ref, w_out_hbm, g1_ref, b1_ref, bias_ref,
                  w_gate_hbm, w_up_hbm, w_down_hbm, g2_ref, b2_ref, o_hbm,
                  q_scr, k_scr, vt_scr, ucarry_scr, merged_scr, z1_scr, hid_scr,
                  w_in_ref, w_out_ref, w_gate_ref, w_up_ref, w_down_ref, stage_in, stage_ff, stage_sq, cast_sem,
                  *, tiles_per_seq, n_tiles, tm):
    z1_scr[...] = jnp.zeros(z1_scr.shape, F32)
    k_scr[...] = jnp.zeros(k_scr.shape, BF16)
    vt_scr[...] = jnp.zeros(vt_scr.shape, BF16)
    ucarry_scr[...] = jnp.zeros(ucarry_scr.shape, F32)
    _cast_weight(w_in_hbm, w_in_ref, stage_in, cast_sem)
    _cast_weight(w_out_hbm, w_out_ref, stage_sq, cast_sem)
    _cast_weight(w_gate_hbm, w_gate_ref, stage_ff, cast_sem)
    _cast_weight(w_up_hbm, w_up_ref, stage_ff, cast_sem)
    _cast_weight(w_down_hbm, w_down_ref, stage_sq, cast_sem)

    def step(x_ref, o_ref):
        _tile_step(sinks_ref, x_ref, conv_w_ref, g1_ref, b1_ref, bias_ref, g2_ref, b2_ref, o_ref,
                   q_scr, k_scr, vt_scr, ucarry_scr, merged_scr, z1_scr, hid_scr,
                   w_in_ref, w_out_ref, w_gate_ref, w_up_ref, w_down_ref, tiles_per_seq=tiles_per_seq)

    in_tile = pl.BlockSpec((tm, D_MODEL), lambda s: (jnp.minimum(s, n_tiles - 1), 0),
                           pipeline_mode=pl.Buffered(3))
    out_tile = pl.BlockSpec((tm, D_MODEL), lambda s: (jnp.maximum(s - 1, 0), 0))
    pltpu.emit_pipeline(step, grid=(n_tiles + 1,), in_specs=[in_tile], out_specs=[out_tile])(x_hbm, o_hbm)


def _tile_step(sinks_ref, x_ref, conv_w_ref, g1_ref, b1_ref, bias_ref, g2_ref, b2_ref, o_ref,
               q_scr, k_scr, vt_scr, ucarry_scr, merged_scr, z1_scr, hid_scr,
               w_in_ref, w_out_ref, w_gate_ref, w_up_ref, w_down_ref, *, tiles_per_seq):
    tm = x_ref.shape[0]
    nblk = tm // BLOCK
    s = pl.program_id(0)
    i = lax.rem(s, tiles_per_seq)
    seq_start = i == 0

    x1 = _layer_norm(z1_scr[...], g1_ref[...], b1_ref[...])
    x1b = x1.astype(BF16)

    def gate_up(c):
        c0 = c * FFN_CHUNK
        gate = _dot(x1b, w_gate_ref[:, c0:c0 + FFN_CHUNK])
        up = _dot(x1b, w_up_ref[:, c0:c0 + FFN_CHUNK])
        hid_scr[:, c0:c0 + FFN_CHUNK] = (gate * jax.nn.sigmoid(gate) * up).astype(BF16)

    k_scr[0:BLOCK, :] = jnp.where(seq_start, jnp.zeros((BLOCK, k_scr.shape[1]), BF16), k_scr[tm:tm + BLOCK, :])
    vt_scr[:, 0:BLOCK] = jnp.where(seq_start, jnp.zeros((vt_scr.shape[0], BLOCK), BF16), vt_scr[:, tm:tm + BLOCK])

    xb = x_ref[...].astype(BF16)
    lane = lax.broadcasted_iota(jnp.int32, (tm, LANES), 1)
    lo = lane < HEAD_DIM

    qkv = _dot(xb, w_in_ref[:, OFF_Q:OFF_CB])
    for blk in range(nblk):
        for pr in range(N_Q_HEADS // 2):
            qp = qkv[blk * BLOCK:(blk + 1) * BLOCK, pr * LANES:(pr + 1) * LANES] * (HEAD_DIM ** -0.5)
            q_scr[blk, pr // 2, (pr % 2) * BLOCK:(pr % 2 + 1) * BLOCK, :] = qp.astype(BF16)

    for t in range(KV_WIDTH // LANES):
        kt = qkv[:, OFF_K + t * LANES:OFF_K + (t + 1) * LANES]
        kr = pltpu.roll(kt, HEAD_DIM, axis=1)
        zero = jnp.zeros_like(kt)
        h0, h1 = 2 * t, 2 * t + 1
        k_scr[BLOCK:, (2 * h0) * LANES:(2 * h0 + 1) * LANES] = jnp.where(lo, kt, zero).astype(BF16)
        k_scr[BLOCK:, (2 * h0 + 1) * LANES:(2 * h0 + 2) * LANES] = jnp.where(lo, zero, kr).astype(BF16)
        k_scr[BLOCK:, (2 * h1) * LANES:(2 * h1 + 1) * LANES] = jnp.where(lo, kr, zero).astype(BF16)
        k_scr[BLOCK:, (2 * h1 + 1) * LANES:(2 * h1 + 2) * LANES] = jnp.where(lo, zero, kt).astype(BF16)
    v_t = qkv[:, OFF_V:OFF_V + KV_WIDTH].T
    for h in range(N_KV_HEADS):
        vh = v_t[h * HEAD_DIM:(h + 1) * HEAD_DIM, :].astype(BF16)
        vt_scr[(2 * h) * HEAD_DIM:(2 * h + 1) * HEAD_DIM, BLOCK:] = vh
        vt_scr[(2 * h + 1) * HEAD_DIM:(2 * h + 2) * HEAD_DIM, BLOCK:] = vh

    for c in range(FFN_HEAD_CHUNKS):
        gate_up(c)

    kj = lax.broadcasted_iota(jnp.int32, (BLOCK, BLOCK), 0)
    qi = lax.broadcasted_iota(jnp.int32, (BLOCK, BLOCK), 1)
    tri = kj <= qi
    top = kj < HEAD_DIM
    valid0 = (kj - qi) <= jnp.where(seq_start, 0, BLOCK)
    row = lax.broadcasted_iota(jnp.int32, (tm, CONV_CHUNK), 0)

    def scores(blk, h):
        r0 = blk * BLOCK
        qg = q_scr[blk, h]
        ke = k_scr[r0:r0 + 2 * BLOCK, (2 * h) * LANES:(2 * h + 1) * LANES]
        ko = k_scr[r0:r0 + 2 * BLOCK, (2 * h + 1) * LANES:(2 * h + 2) * LANES]
        se = _dot_nt(ke, qg)
        so = _dot_nt(ko, qg)
        return se, so

    def attend(blk, h, se, so):
        r0 = blk * BLOCK
        probs, inv = [], []
        for g, (sc, half) in enumerate(((se, 0), (so, 0), (se, 1), (so, 1))):
            head = GQA_GROUP * h + g
            s_prev = sc[0:BLOCK, half * BLOCK:(half + 1) * BLOCK]
            s_cur = sc[BLOCK:2 * BLOCK, half * BLOCK:(half + 1) * BLOCK]
            sm = jnp.where(tri, s_cur, s_prev) + bias_ref[head]
            if blk == 0:
                sm = jnp.where(valid0, sm, NEG_INF)
            sink = sinks_ref[head]
            m = jnp.maximum(jnp.max(sm, axis=0, keepdims=True), sink)
            p = jnp.exp(sm - m)
            denom = jnp.sum(p, axis=0, keepdims=True) + jnp.exp(sink - m)
            inv.append(1.0 / denom)
            zero = jnp.zeros_like(p)
            probs.append(jnp.concatenate([jnp.where(tri, zero, p), jnp.where(tri, p, zero)],
                                         axis=0).astype(BF16))
        p_t = jnp.concatenate(probs, axis=1)
        vt = vt_scr[(2 * h) * HEAD_DIM:(2 * h + 2) * HEAD_DIM, r0:r0 + 2 * BLOCK]
        o_t = _dot(vt, p_t) * jnp.concatenate(inv, axis=1)
        pairs = []
        for pr in range(2):
            oe = o_t[:, (2 * pr) * BLOCK:(2 * pr + 1) * BLOCK]
            oo = o_t[:, (2 * pr + 1) * BLOCK:(2 * pr + 2) * BLOCK]
            pairs.append(jnp.where(top, oe, oo).T)
        return jnp.concatenate(pairs, axis=1)

    def proj(off, h):
        c0 = off + h * CONV_CHUNK
        return _dot(xb, w_in_ref[:, c0:c0 + CONV_CHUNK])

    units = [(h, blk) for h in range(N_KV_HEADS) for blk in range(nblk)]
    assert nblk == 2
    sc = scores(0, 0)
    for n, (h, blk) in enumerate(units):
        c0 = h * CONV_CHUNK
        if blk == 0:
            cb, cc, ch = proj(OFF_CB, h), proj(OFF_CC, h), proj(OFF_CH, h)
            y_blocks = []
        else:
            gc, ga = proj(OFF_GC, h), proj(OFF_GA, h)
        gate_up(FFN_HEAD_CHUNKS + n)
        if n + 1 < len(units):
            sc_next = scores(units[n + 1][1], units[n + 1][0])
        y_blocks.append(attend(blk, h, *sc))
        sc = sc_next
        if blk == 0:
            continue
        y_attn = jnp.concatenate(y_blocks, axis=0)
        u = cc * ch
        zrow = jnp.zeros((1, CONV_CHUNK), F32)
        last = jnp.where(seq_start, zrow, ucarry_scr[SUBLANES - 1:SUBLANES, c0:c0 + CONV_CHUNK])
        prev = jnp.where(seq_start, zrow, ucarry_scr[SUBLANES - 2:SUBLANES - 1, c0:c0 + CONV_CHUNK])
        u1 = jnp.where(row == 0, last, pltpu.roll(u, 1, axis=0))
        u2 = jnp.where(row == 0, prev, jnp.where(row == 1, last, pltpu.roll(u, 2, axis=0)))
        w = conv_w_ref[:, c0:c0 + CONV_CHUNK]
        y = w[0:1, :] * u2 + w[1:2, :] * u1 + w[2:3, :] * u
        ucarry_scr[:, c0:c0 + CONV_CHUNK] = u[tm - SUBLANES:tm, :]
        merged = jax.nn.sigmoid(gc) * (cb * y) + jax.nn.sigmoid(ga) * y_attn
        merged_scr[:, c0:c0 + CONV_CHUNK] = merged.astype(BF16)

    ffn_a = _dot(hid_scr[0:BLOCK, :], w_down_ref[...])
    ya = _layer_norm(ALPHA * x1[0:BLOCK, :] + ffn_a, g2_ref[...], b2_ref[...])
    o_ref[0:BLOCK, :] = ya
    bits = lax.bitcast_convert_type(ya, jnp.uint32)
    folded = bits[:, 0:LANES]
    for c in range(1, D_MODEL // LANES):
        folded = folded | bits[:, c * LANES:(c + 1) * LANES]
    plus_zero = lax.bitcast_convert_type((folded >> 16) >> 16, F32)
    ffn_b = _dot(hid_scr[BLOCK:2 * BLOCK, :], w_down_ref[...])
    zb = ALPHA * x1[BLOCK:2 * BLOCK, :] + ffn_b
    zb = jnp.concatenate([zb[:, 0:LANES] + plus_zero, zb[:, LANES:]], axis=1)
    o_ref[BLOCK:2 * BLOCK, :] = _layer_norm(zb, g2_ref[...], b2_ref[...])

    mix = _dot(merged_scr[...], w_out_ref[...])
    z1_scr[...] = ALPHA * x_ref[...] + mix


def _block_call(x, sinks, w_in, conv_w, w_out, g1, b1, w_gate, w_up, w_down, g2, b2, seq):
    tokens = x.shape[0]
    tm = TILE
    nblk = tm // BLOCK
    n_tiles = tokens // tm
    in_hbm = pl.BlockSpec(memory_space=pl.ANY)
    in_vmem = pl.BlockSpec(memory_space=pltpu.VMEM)
    return pl.pallas_call(
        functools.partial(_block_kernel, tiles_per_seq=seq // tm, n_tiles=n_tiles, tm=tm),
        in_specs=[
            pl.BlockSpec(memory_space=pltpu.SMEM),
            in_hbm,
            in_hbm,
            in_vmem,
            in_hbm,
            in_vmem,
            in_vmem,
            in_vmem,
            in_hbm,
            in_hbm,
            in_hbm,
            in_vmem,
            in_vmem,
        ],
        out_specs=in_hbm,
        out_shape=jax.ShapeDtypeStruct(x.shape, F32),
        scratch_shapes=[
            pltpu.VMEM((nblk, N_KV_HEADS, 2 * BLOCK, LANES), BF16),
            pltpu.VMEM((tm + BLOCK, 2 * N_KV_HEADS * LANES), BF16),
            pltpu.VMEM((N_KV_HEADS * 2 * HEAD_DIM, tm + BLOCK), BF16),
            pltpu.VMEM((SUBLANES, D_MODEL), F32),
            pltpu.VMEM((tm, D_MODEL), BF16),
            pltpu.VMEM((tm, D_MODEL), F32),
            pltpu.VMEM((tm, D_FF), BF16),
            pltpu.VMEM((D_MODEL, IN_WIDTH), BF16),
            pltpu.VMEM((D_MODEL, D_MODEL), BF16),
            pltpu.VMEM((D_MODEL, D_FF), BF16),
            pltpu.VMEM((D_MODEL, D_FF), BF16),
            pltpu.VMEM((D_FF, D_MODEL), BF16),
            pltpu.VMEM((CAST_SLOTS, CAST_ROWS_IN, IN_WIDTH), F32),
            pltpu.VMEM((CAST_SLOTS, CAST_ROWS_FF, D_FF), F32),
            pltpu.VMEM((CAST_SLOTS, CAST_ROWS_SQ, D_MODEL), F32),
            pltpu.SemaphoreType.DMA((CAST_SLOTS,)),
        ],
        compiler_params=pltpu.CompilerParams(vmem_limit_bytes=VMEM_LIMIT),
        name="block",
    )(sinks, x, w_in, conv_w, w_out, g1, b1, jnp.asarray(_alibi_bias()), w_gate, w_up, w_down, g2, b2)


def kernel(x, w_in, conv_w, attn_sinks, w_out, ln1_g, ln1_b, w_gate, w_up, w_down, ln2_g, ln2_b):
    bsz, seq, _ = x.shape
    assert seq % TILE == 0 and TILE % BLOCK == 0
    x = x.reshape(bsz * seq, D_MODEL)
    for l in range(DEPTH):
        x = _block_call(x, attn_sinks[l], w_in[l], conv_w[l], w_out[l],
                        ln1_g[l][None, :], ln1_b[l][None, :], w_gate[l], w_up[l], w_down[l],
                        ln2_g[l][None, :], ln2_b[l][None, :], seq)
    return x.reshape(bsz, seq, D_MODEL)
```

```python
import functools

import numpy as np

import jax
import jax.numpy as jnp
from jax import lax
from jax.experimental import pallas as pl
from jax.experimental.pallas import tpu as pltpu

D_MODEL = 1024
HEAD_DIM = 64
N_Q_HEADS = D_MODEL // HEAD_DIM
N_KV_HEADS = N_Q_HEADS // 4
GQA_GROUP = N_Q_HEADS // N_KV_HEADS
KV_WIDTH = N_KV_HEADS * HEAD_DIM
WINDOW = 128
BLOCK = 128
CONV_K = 3
D_FF = 2816
DEPTH = 1
ALPHA = (2.0 * DEPTH) ** 0.25
LN_EPS = 1e-5
NEG_INF = -1e30

OFF_Q = 0
OFF_K = OFF_Q + D_MODEL
OFF_V = OFF_K + KV_WIDTH
OFF_CB = OFF_V + KV_WIDTH
OFF_CC = OFF_CB + D_MODEL
OFF_CH = OFF_CC + D_MODEL
OFF_GC = OFF_CH + D_MODEL
OFF_GA = OFF_GC + D_MODEL
IN_WIDTH = OFF_GA + D_MODEL

LANES = 128
SUBLANES = 8
TILE = 256
CONV_CHUNK = GQA_GROUP * HEAD_DIM
FFN_CHUNK = 256
N_FFN_CHUNKS = D_FF // FFN_CHUNK
FFN_HEAD_CHUNKS = N_FFN_CHUNKS - 2 * N_KV_HEADS
VMEM_LIMIT = 58 * 1024 * 1024
CAST_SLOTS = 3
CAST_ROWS_IN = 64
CAST_ROWS_FF = 128
CAST_ROWS_SQ = 256

F32 = jnp.float32
BF16 = jnp.bfloat16


def _alibi_bias():
    slopes = np.array([2.0 ** (-8.0 * (h + 1) / N_Q_HEADS) for h in range(N_Q_HEADS)], dtype=np.float32)
    kj = np.arange(BLOCK)[:, None]
    qi = np.arange(BLOCK)[None, :]
    dist = ((qi - kj) % BLOCK).astype(np.float32)
    return -slopes[:, None, None] * dist[None]


def _dot(a, b):
    return jnp.dot(a, b, preferred_element_type=F32)


def _dot_nt(a, b):
    return lax.dot_general(a, b, (((1,), (1,)), ((), ())), preferred_element_type=F32)


def _layer_norm(z, g, b):
    mu = jnp.mean(z, axis=-1, keepdims=True)
    d = z - mu
    var = jnp.mean(d * d, axis=-1, keepdims=True)
    return d * lax.rsqrt(var + LN_EPS) * g + b


def _cast_weight(src_hbm, dst_scr, stage, sem):
    slots, rows = stage.shape[0], stage.shape[1]
    n = src_hbm.shape[0] // rows
    assert n * rows == src_hbm.shape[0] and stage.shape[2] == src_hbm.shape[1] and n >= slots

    def chunk_copy(k):
        slot = k % slots
        return pltpu.make_async_copy(src_hbm.at[pl.ds(k * rows, rows), :], stage.at[slot], sem.at[slot])

    for k in range(slots - 1):
        chunk_copy(k).start(priority=k % 2)
    for k in range(n):
        if k + slots - 1 < n:
            chunk_copy(k + slots - 1).start(priority=(k + slots - 1) % 2)
        chunk_copy(k).wait()
        dst_scr[pl.ds(k * rows, rows), :] = stage[k % slots].astype(BF16)


def _block_kernel(sinks_ref, x_hbm, w_in_hbm, conv_w_ref, w_out_hbm, g1_ref, b1_ref, bias_ref,
                  w_gate_hbm, w_up_hbm, w_down_hbm, g2_ref, b2_ref, o_hbm,
                  q_scr, k_scr, vt_scr, ucarry_scr, merged_scr, z1_scr, hid_scr,
                  w_in_ref, w_out_ref, w_gate_ref, w_up_ref, w_down_ref, stage_in, stage_ff, stage_sq, cast_sem,
                  *, tiles_per_seq, n_tiles, tm):
    z1_scr[...] = jnp.zeros(z1_scr.shape, F32)
    k_scr[...] = jnp.zeros(k_scr.shape, BF16)
    vt_scr[...] = jnp.zeros(vt_scr.shape, BF16)
    ucarry_scr[...] = jnp.zeros(ucarry_scr.shape, F32)
    _cast_weight(w_in_hbm, w_in_ref, stage_in, cast_sem)
    _cast_weight(w_out_hbm, w_out_ref, stage_sq, cast_sem)
    _cast_weight(w_gate_hbm, w_gate_ref, stage_ff, cast_sem)
    _cast_weight(w_up_hbm, w_up_ref, stage_ff, cast_sem)
    _cast_weight(w_down_hbm, w_down_ref, stage_sq, cast_sem)

    def step(x_ref, o_ref):
        _tile_step(sinks_ref, x_ref, conv_w_ref, g1_ref, b1_ref, bias_ref, g2_ref, b2_ref, o_ref,
                   q_scr, k_scr, vt_scr, ucarry_scr, merged_scr, z1_scr, hid_scr,
                   w_in_ref, w_out_ref, w_gate_ref, w_up_ref, w_down_ref, tiles_per_seq=tiles_per_seq)

    in_tile = pl.BlockSpec((tm, D_MODEL), lambda s: (jnp.minimum(s, n_tiles - 1), 0))
    out_tile = pl.BlockSpec((tm, D_MODEL), lambda s: (jnp.maximum(s - 1, 0), 0))
    pltpu.emit_pipeline(step, grid=(n_tiles + 1,), in_specs=[in_tile], out_specs=[out_tile])(x_hbm, o_hbm)


def _tile_step(sinks_ref, x_ref, conv_w_ref, g1_ref, b1_ref, bias_ref, g2_ref, b2_ref, o_ref,
               q_scr, k_scr, vt_scr, ucarry_scr, merged_scr, z1_scr, hid_scr,
               w_in_ref, w_out_ref, w_gate_ref, w_up_ref, w_down_ref, *, tiles_per_seq):
    tm = x_ref.shape[0]
    nblk = tm // BLOCK
    s = pl.program_id(0)
    i = lax.rem(s, tiles_per_seq)
    seq_start = i == 0

    x1 = _layer_norm(z1_scr[...], g1_ref[...], b1_ref[...])
    x1b = x1.astype(BF16)

    def gate_up(c):
        c0 = c * FFN_CHUNK
        gate = _dot(x1b, w_gate_ref[:, c0:c0 + FFN_CHUNK])
        up = _dot(x1b, w_up_ref[:, c0:c0 + FFN_CHUNK])
        hid_scr[:, c0:c0 + FFN_CHUNK] = (gate * jax.nn.sigmoid(gate) * up).astype(BF16)

    k_scr[0:BLOCK, :] = jnp.where(seq_start, jnp.zeros((BLOCK, k_scr.shape[1]), BF16), k_scr[tm:tm + BLOCK, :])
    vt_scr[:, 0:BLOCK] = jnp.where(seq_start, jnp.zeros((vt_scr.shape[0], BLOCK), BF16), vt_scr[:, tm:tm + BLOCK])

    xb = x_ref[...].astype(BF16)
    lane = lax.broadcasted_iota(jnp.int32, (tm, LANES), 1)
    lo = lane < HEAD_DIM

    qkv = _dot(xb, w_in_ref[:, OFF_Q:OFF_CB])
    for blk in range(nblk):
        for pr in range(N_Q_HEADS // 2):
            qp = qkv[blk * BLOCK:(blk + 1) * BLOCK, pr * LANES:(pr + 1) * LANES] * (HEAD_DIM ** -0.5)
            q_scr[blk, pr // 2, (pr % 2) * BLOCK:(pr % 2 + 1) * BLOCK, :] = qp.astype(BF16)

    for t in range(KV_WIDTH // LANES):
        kt = qkv[:, OFF_K + t * LANES:OFF_K + (t + 1) * LANES]
        kr = pltpu.roll(kt, HEAD_DIM, axis=1)
        zero = jnp.zeros_like(kt)
        h0, h1 = 2 * t, 2 * t + 1
        k_scr[BLOCK:, (2 * h0) * LANES:(2 * h0 + 1) * LANES] = jnp.where(lo, kt, zero).astype(BF16)
        k_scr[BLOCK:, (2 * h0 + 1) * LANES:(2 * h0 + 2) * LANES] = jnp.where(lo, zero, kr).astype(BF16)
        k_scr[BLOCK:, (2 * h1) * LANES:(2 * h1 + 1) * LANES] = jnp.where(lo, kr, zero).astype(BF16)
        k_scr[BLOCK:, (2 * h1 + 1) * LANES:(2 * h1 + 2) * LANES] = jnp.where(lo, zero, kt).astype(BF16)
    v_t = qkv[:, OFF_V:OFF_V + KV_WIDTH].T
    for h in range(N_KV_HEADS):
        vh = v_t[h * HEAD_DIM:(h + 1) * HEAD_DIM, :].astype(BF16)
        vt_scr[(2 * h) * HEAD_DIM:(2 * h + 1) * HEAD_DIM, BLOCK:] = vh
        vt_scr[(2 * h + 1) * HEAD_DIM:(2 * h + 2) * HEAD_DIM, BLOCK:] = vh

    for c in range(FFN_HEAD_CHUNKS):
        gate_up(c)

    kj = lax.broadcasted_iota(jnp.int32, (BLOCK, BLOCK), 0)
    qi = lax.broadcasted_iota(jnp.int32, (BLOCK, BLOCK), 1)
    tri = kj <= qi
    top = kj < HEAD_DIM
    valid0 = (kj - qi) <= jnp.where(seq_start, 0, BLOCK)
    row = lax.broadcasted_iota(jnp.int32, (tm, CONV_CHUNK), 0)

    def scores(blk, h):
        r0 = blk * BLOCK
        qg = q_scr[blk, h]
        ke = k_scr[r0:r0 + 2 * BLOCK, (2 * h) * LANES:(2 * h + 1) * LANES]
        ko = k_scr[r0:r0 + 2 * BLOCK, (2 * h + 1) * LANES:(2 * h + 2) * LANES]
        se = _dot_nt(ke, qg)
        so = _dot_nt(ko, qg)
        return se, so

    def attend(blk, h, se, so):
        r0 = blk * BLOCK
        probs, inv = [], []
        for g, (sc, half) in enumerate(((se, 0), (so, 0), (se, 1), (so, 1))):
            head = GQA_GROUP * h + g
            s_prev = sc[0:BLOCK, half * BLOCK:(half + 1) * BLOCK]
            s_cur = sc[BLOCK:2 * BLOCK, half * BLOCK:(half + 1) * BLOCK]
            sm = jnp.where(tri, s_cur, s_prev) + bias_ref[head]
            if blk == 0:
                sm = jnp.where(valid0, sm, NEG_INF)
            sink = sinks_ref[head]
            m = jnp.maximum(jnp.max(sm, axis=0, keepdims=True), sink)
            p = jnp.exp(sm - m)
            denom = jnp.sum(p, axis=0, keepdims=True) + jnp.exp(sink - m)
            inv.append(1.0 / denom)
            zero = jnp.zeros_like(p)
            probs.append(jnp.concatenate([jnp.where(tri, zero, p), jnp.where(tri, p, zero)],
                                         axis=0).astype(BF16))
        p_t = jnp.concatenate(probs, axis=1)
        vt = vt_scr[(2 * h) * HEAD_DIM:(2 * h + 2) * HEAD_DIM, r0:r0 + 2 * BLOCK]
        o_t = _dot(vt, p_t) * jnp.concatenate(inv, axis=1)
        pairs = []
        for pr in range(2):
            oe = o_t[:, (2 * pr) * BLOCK:(2 * pr + 1) * BLOCK]
            oo = o_t[:, (2 * pr + 1) * BLOCK:(2 * pr + 2) * BLOCK]
            pairs.append(jnp.where(top, oe, oo).T)
        return jnp.concatenate(pairs, axis=1)

    def proj(off, h):
        c0 = off + h * CONV_CHUNK
        return _dot(xb, w_in_ref[:, c0:c0 + CONV_CHUNK])

    units = [(h, blk) for h in range(N_KV_HEADS) for blk in range(nblk)]
    assert nblk == 2
    sc = scores(0, 0)
    for n, (h, blk) in enumerate(units):
        c0 = h * CONV_CHUNK
        if blk == 0:
            cb, cc, ch = proj(OFF_CB, h), proj(OFF_CC, h), proj(OFF_CH, h)
            y_blocks = []
        else:
            gc, ga = proj(OFF_GC, h), proj(OFF_GA, h)
        gate_up(FFN_HEAD_CHUNKS + n)
        if n + 1 < len(units):
            sc_next = scores(units[n + 1][1], units[n + 1][0])
        y_blocks.append(attend(blk, h, *sc))
        sc = sc_next
        if blk == 0:
            continue
        y_attn = jnp.concatenate(y_blocks, axis=0)
        u = cc * ch
        zrow = jnp.zeros((1, CONV_CHUNK), F32)
        last = jnp.where(seq_start, zrow, ucarry_scr[SUBLANES - 1:SUBLANES, c0:c0 + CONV_CHUNK])
        prev = jnp.where(seq_start, zrow, ucarry_scr[SUBLANES - 2:SUBLANES - 1, c0:c0 + CONV_CHUNK])
        u1 = jnp.where(row == 0, last, pltpu.roll(u, 1, axis=0))
        u2 = jnp.where(row == 0, prev, jnp.where(row == 1, last, pltpu.roll(u, 2, axis=0)))
        w = conv_w_ref[:, c0:c0 + CONV_CHUNK]
        y = w[0:1, :] * u2 + w[1:2, :] * u1 + w[2:3, :] * u
        ucarry_scr[:, c0:c0 + CONV_CHUNK] = u[tm - SUBLANES:tm, :]
        merged = jax.nn.sigmoid(gc) * (cb * y) + jax.nn.sigmoid(ga) * y_attn
        merged_scr[:, c0:c0 + CONV_CHUNK] = merged.astype(BF16)

    ffn_a = _dot(hid_scr[0:BLOCK, :], w_down_ref[...])
    ya = _layer_norm(ALPHA * x1[0:BLOCK, :] + ffn_a, g2_ref[...], b2_ref[...])
    o_ref[0:BLOCK, :] = ya
    bits = lax.bitcast_convert_type(ya, jnp.uint32)
    folded = bits[:, 0:LANES]
    for c in range(1, D_MODEL // LANES):
        folded = folded | bits[:, c * LANES:(c + 1) * LANES]
    plus_zero = lax.bitcast_convert_type((folded >> 16) >> 16, F32)
    ffn_b = _dot(hid_scr[BLOCK:2 * BLOCK, :], w_down_ref[...])
    zb = ALPHA * x1[BLOCK:2 * BLOCK, :] + ffn_b
    zb = jnp.concatenate([zb[:, 0:LANES] + plus_zero, zb[:, LANES:]], axis=1)
    o_ref[BLOCK:2 * BLOCK, :] = _layer_norm(zb, g2_ref[...], b2_ref[...])

    mix = _dot(merged_scr[...], w_out_ref[...])
    z1_scr[...] = ALPHA * x_ref[...] + mix


def _block_call(x, sinks, w_in, conv_w, w_out, g1, b1, w_gate, w_up, w_down, g2, b2, seq):
    tokens = x.shape[0]
    tm = TILE
    nblk = tm // BLOCK
    n_tiles = tokens // tm
    in_hbm = pl.BlockSpec(memory_space=pl.ANY)
    in_vmem = pl.BlockSpec(memory_space=pltpu.VMEM)
    return pl.pallas_call(
        functools.partial(_block_kernel, tiles_per_seq=seq // tm, n_tiles=n_tiles, tm=tm),
        in_specs=[
            pl.BlockSpec(memory_space=pltpu.SMEM),
            in_hbm,
            in_hbm,
            in_vmem,
            in_hbm,
            in_vmem,
            in_vmem,
            in_vmem,
            in_hbm,
            in_hbm,
            in_hbm,
            in_vmem,
            in_vmem,
        ],
        out_specs=in_hbm,
        out_shape=jax.ShapeDtypeStruct(x.shape, F32),
        scratch_shapes=[
            pltpu.VMEM((nblk, N_KV_HEADS, 2 * BLOCK, LANES), BF16),
            pltpu.VMEM((tm + BLOCK, 2 * N_KV_HEADS * LANES), BF16),
            pltpu.VMEM((N_KV_HEADS * 2 * HEAD_DIM, tm + BLOCK), BF16),
            pltpu.VMEM((SUBLANES, D_MODEL), F32),
            pltpu.VMEM((tm, D_MODEL), BF16),
            pltpu.VMEM((tm, D_MODEL), F32),
            pltpu.VMEM((tm, D_FF), BF16),
            pltpu.VMEM((D_MODEL, IN_WIDTH), BF16),
            pltpu.VMEM((D_MODEL, D_MODEL), BF16),
            pltpu.VMEM((D_MODEL, D_FF), BF16),
            pltpu.VMEM((D_MODEL, D_FF), BF16),
            pltpu.VMEM((D_FF, D_MODEL), BF16),
            pltpu.VMEM((CAST_SLOTS, CAST_ROWS_IN, IN_WIDTH), F32),
            pltpu.VMEM((CAST_SLOTS, CAST_ROWS_FF, D_FF), F32),
            pltpu.VMEM((CAST_SLOTS, CAST_ROWS_SQ, D_MODEL), F32),
            pltpu.SemaphoreType.DMA((CAST_SLOTS,)),
        ],
        compiler_params=pltpu.CompilerParams(vmem_limit_bytes=VMEM_LIMIT),
        name="block",
    )(sinks, x, w_in, conv_w, w_out, g1, b1, jnp.asarray(_alibi_bias()), w_gate, w_up, w_down, g2, b2)


def kernel(x, w_in, conv_w, attn_sinks, w_out, ln1_g, ln1_b, w_gate, w_up, w_down, ln2_g, ln2_b):
    bsz, seq, _ = x.shape
    assert seq % TILE == 0 and TILE % BLOCK == 0
    x = x.reshape(bsz * seq, D_MODEL)
    for l in range(DEPTH):
        x = _block_call(x, attn_sinks[l], w_in[l], conv_w[l], w_out[l],
                        ln1_g[l][None, :], ln1_b[l][None, :], w_gate[l], w_up[l], w_down[l],
                        ln2_g[l][None, :], ln2_b[l][None, :], seq)
    return x.reshape(bsz, seq, D_MODEL)
```
